```python
import jax, jax.numpy as jnp
from jax import lax
import numpy as np

D_MODEL = 1024
BATCH = 4
SEQ = 4096
DEPTH = 4
DEC_BATCH = 32
DEC_SEQ = 8
PAST_LEN = 8192
PAGE_SIZE = 128

N_A_LAYERS = DEPTH // 2
N_B_LAYERS = DEPTH - N_A_LAYERS
POOL_WINDOWS = (2, 4, 8, 16)
N_POOL_GROUPS = len(POOL_WINDOWS)
POOL_GROUP = D_MODEL // N_POOL_GROUPS
POOL_HIST = max(POOL_WINDOWS) - 1
WINDOWS = (128, 512, 2048)
DILATIONS = (1, 4, 16)
N_BRANCH = len(WINDOWS)
HEAD_DIM = 64
N_HEADS = D_MODEL // HEAD_DIM
ATTN_WIDTH = N_HEADS * HEAD_DIM
D_FF = -(-8 * D_MODEL // (3 * 256)) * 256
ROPE_THETA = 10000.0
EPS = 1e-6

kernel_name = "yoco_pool_dilated_swa_step"

F32 = jnp.float32


def _rmsnorm(x, g):
    xf = x.astype(F32)
    y = xf * lax.rsqrt(jnp.mean(xf * xf, axis=-1, keepdims=True) + EPS)
    return (y * g.astype(F32)).astype(x.dtype)


def _rope_tables(pos):
    inv = jnp.power(ROPE_THETA, -jnp.arange(0, HEAD_DIM, 2, dtype=F32) / HEAD_DIM)
    ang = pos.astype(F32)[:, None] * inv[None, :]
    ang = jnp.concatenate([ang, ang], axis=-1)
    return jnp.cos(ang), jnp.sin(ang)


def _rope(x, cos, sin):
    xf = x.astype(F32)
    x1, x2 = jnp.split(xf, 2, axis=-1)
    rot = jnp.concatenate([-x2, x1], axis=-1)
    c = cos[None, :, None, None, :]
    s = sin[None, :, None, None, :]
    return (xf * c + rot * s).astype(x.dtype)


def _swiglu(h, w_gate, w_up, w_down):
    return (jax.nn.silu(h @ w_gate) * (h @ w_up)) @ w_down


def _pool_mixer(u_ext, offset, w_pool, scale):
    L = u_ext.shape[1]
    idx = jnp.arange(offset, L)
    c = jnp.pad(jnp.cumsum(u_ext.astype(F32), axis=1), ((0, 0), (1, 0), (0, 0)))
    hi = c[:, offset + 1:]
    u = u_ext[:, offset:].astype(F32)
    outs = []
    for g, win in enumerate(POOL_WINDOWS):
        sl = slice(g * POOL_GROUP, (g + 1) * POOL_GROUP)
        lo = jnp.take(c[:, :, sl], jnp.maximum(idx + 1 - win, 0), axis=1)
        cnt = jnp.minimum(idx + 1, win).astype(F32)[None, :, None]
        d = (hi[:, :, sl] - lo) / cnt - u[:, :, sl]
        outs.append(d.astype(u_ext.dtype) @ w_pool[g])
    return jnp.concatenate(outs, axis=-1) * scale


def _shared_kv(x, kv_norm, w_kv, k_norm, cos, sin):
    B, S, _ = x.shape
    kv = (_rmsnorm(x, kv_norm) @ w_kv).reshape(B, S, 2, N_BRANCH, N_HEADS, HEAD_DIM)
    k = _rope(_rmsnorm(kv[:, :, 0], k_norm[:, None, :]), cos, sin)
    return k, kv[:, :, 1]


def _queries(x, b_norm, w_q, q_norm, cos, sin):
    B, S, _ = x.shape
    q = (_rmsnorm(x, b_norm) @ w_q).reshape(B, S, N_BRANCH, N_HEADS, HEAD_DIM)
    return _rope(_rmsnorm(q, q_norm[:, None, :]), cos, sin)


def _dilated_prompt(q, k, v, window, dil):
    B, S, H, D = q.shape
    blk = window // dil
    L = S // dil
    nb = -(-L // blk)
    Lp = nb * blk

    def to_blocks(a):
        a = a.reshape(B, L, dil, H, D).transpose(0, 2, 1, 3, 4)
        a = jnp.pad(a, ((0, 0), (0, 0), (0, Lp - L), (0, 0), (0, 0)))
        return a.reshape(B, dil, nb, blk, H, D)

    def with_prev(a):
        prev = jnp.pad(a, ((0, 0), (0, 0), (1, 0), (0, 0), (0, 0), (0, 0)))[:, :, :-1]
        return jnp.concatenate([prev, a], axis=3)

    qb = to_blocks(q)
    kc = with_prev(to_blocks(k))
    vc = with_prev(to_blocks(v))
    s = jnp.einsum('brnqhd,brnkhd->brnhqk', qb, kc, preferred_element_type=F32) * (HEAD_DIM ** -0.5)
    qi = jnp.arange(blk)[:, None]
    ki = jnp.arange(2 * blk)[None, :]
    rel = qi + blk - ki
    band = (rel >= 0) & (rel <= blk)
    kidx = jnp.arange(nb)[:, None] * blk - blk + jnp.arange(2 * blk)[None, :]
    mask = band[None, None, :, :] & (kidx >= 0)[:, None, None, :]
    s = jnp.where(mask, s, -jnp.inf)
    lse = jax.nn.logsumexp(s, axis=-1)
    p = jnp.exp(s - lse[..., None])
    o = jnp.einsum('brnhqk,brnkhd->brnqhd', p.astype(v.dtype), vc)
    o = o.reshape(B, dil, Lp, H, D)[:, :, :L].transpose(0, 2, 1, 3, 4).reshape(B, S, H, D)
    lse = lse.transpose(0, 1, 2, 4, 3).reshape(B, dil, Lp, H)[:, :, :L]
    lse = lse.transpose(0, 2, 1, 3).reshape(B, S, H)
    return o, lse


def _dilated_sample(q, kc, vc, hist, window, dil):
    T = q.shape[1]
    steps = window // dil
    idx = hist + jnp.arange(T)[:, None] - dil * jnp.arange(steps + 1)[None, :]
    valid = idx >= 0
    idx = jnp.maximum(idx, 0)
    kg = kc[:, idx]
    vg = vc[:, idx]
    s = jnp.einsum('nthd,ntkhd->nthk', q, kg, preferred_element_type=F32) * (HEAD_DIM ** -0.5)
    s = jnp.where(valid[None, :, None, :], s, -jnp.inf)
    lse = jax.nn.logsumexp(s, axis=-1)
    p = jnp.exp(s - lse[..., None])
    o = jnp.einsum('nthk,ntkhd->nthd', p.astype(vc.dtype), vg)
    return o, lse


def _merge(outs, lses, w_o):
    wts = jax.nn.softmax(jnp.stack(lses, axis=0), axis=0)
    o = jnp.sum(wts[..., None] * jnp.stack(outs, axis=0).astype(F32), axis=0)
    B, S = o.shape[:2]
    return o.reshape(B, S, ATTN_WIDTH).astype(outs[0].dtype) @ w_o


def setup_inputs(seed: int = 0) -> dict:
    key = jax.random.key(seed)
    ks = jax.random.split(key, 24)

    def nrm(k, shape, scale=1.0):
        return jax.random.normal(k, shape, F32) * scale

    def gain(k, shape):
        return 1.0 + 0.02 * jax.random.normal(k, shape, F32)

    hist = [min(w, PAST_LEN) for w in WINDOWS]
    return {
        "x_prompt": nrm(ks[0], (BATCH, SEQ, D_MODEL)),
        "x_sample": nrm(ks[1], (DEC_BATCH, DEC_SEQ, D_MODEL)),
        "state_pool": nrm(ks[2], (DEC_BATCH, N_A_LAYERS, POOL_HIST, D_MODEL)),
        "cache_kv_w128": nrm(ks[3], (DEC_BATCH, hist[0], 2, N_HEADS, HEAD_DIM)),
        "cache_kv_w512": nrm(ks[4], (DEC_BATCH, hist[1], 2, N_HEADS, HEAD_DIM)),
        "cache_kv_w2048": nrm(ks[5], (DEC_BATCH, hist[2], 2, N_HEADS, HEAD_DIM)),
        "a_norm": gain(ks[6], (N_A_LAYERS, D_MODEL)),
        "pool_w": nrm(ks[7], (N_A_LAYERS, N_POOL_GROUPS, POOL_GROUP, POOL_GROUP), POOL_GROUP ** -0.5),
        "pool_scale": gain(ks[8], (N_A_LAYERS, D_MODEL)),
        "kv_norm": gain(ks[9], (D_MODEL,)),
        "w_kv": nrm(ks[10], (D_MODEL, 2 * N_BRANCH * ATTN_WIDTH), D_MODEL ** -0.5),
        "k_norm": gain(ks[11], (N_BRANCH, HEAD_DIM)),
        "b_norm": gain(ks[12], (N_B_LAYERS, D_MODEL)),
        "w_q": nrm(ks[13], (N_B_LAYERS, D_MODEL, N_BRANCH * ATTN_WIDTH), D_MODEL ** -0.5),
        "q_norm": gain(ks[14], (N_B_LAYERS, N_BRANCH, HEAD_DIM)),
        "w_o": nrm(ks[15], (N_B_LAYERS, ATTN_WIDTH, D_MODEL), ATTN_WIDTH ** -0.5),
        "ffn_norm": gain(ks[16], (DEPTH, D_MODEL)),
        "w_gate": nrm(ks[17], (DEPTH, D_MODEL, D_FF), D_MODEL ** -0.5),
        "w_up": nrm(ks[18], (DEPTH, D_MODEL, D_FF), D_MODEL ** -0.5),
        "w_down": nrm(ks[19], (DEPTH, D_FF, D_MODEL), D_FF ** -0.5),
    }


def reference(x_prompt, x_sample, state_pool, cache_kv_w128, cache_kv_w512, cache_kv_w2048,
              a_norm, pool_w, pool_scale, kv_norm, w_kv, k_norm, b_norm, w_q, q_norm, w_o,
              ffn_norm, w_gate, w_up, w_down):
    S_p = x_prompt.shape[1]
    T = x_sample.shape[1]
    cos_p, sin_p = _rope_tables(jnp.arange(S_p))
    cos_s, sin_s = _rope_tables(PAST_LEN + jnp.arange(T))
    caches = (cache_kv_w128, cache_kv_w512, cache_kv_w2048)

    xp, xs = x_prompt, x_sample
    pool_p, pool_s = [], []
    kp = vp = None
    kv_cat = []
    kv_new_p = []
    for layer in range(DEPTH):
        if layer < N_A_LAYERS:
            i = layer
            up = _rmsnorm(xp, a_norm[i])
            us = _rmsnorm(xs, a_norm[i])
            us_ext = jnp.concatenate([state_pool[:, i].astype(us.dtype), us], axis=1)
            xp = xp + _pool_mixer(up, 0, pool_w[i], pool_scale[i])
            xs = xs + _pool_mixer(us_ext, POOL_HIST, pool_w[i], pool_scale[i])
            pool_p.append(up[:, -POOL_HIST:])
            pool_s.append(us_ext[:, -POOL_HIST:])
        else:
            j = layer - N_A_LAYERS
            if j == 0:
                kp, vp = _shared_kv(xp, kv_norm, w_kv, k_norm, cos_p, sin_p)
                ks_, vs_ = _shared_kv(xs, kv_norm, w_kv, k_norm, cos_s, sin_s)
                for g in range(N_BRANCH):
                    new_rows = jnp.stack([ks_[:, :, g], vs_[:, :, g]], axis=2)
                    kv_cat.append(jnp.concatenate([caches[g].astype(new_rows.dtype), new_rows], axis=1))
                    keep = min(WINDOWS[g], S_p)
                    kv_new_p.append(jnp.stack([kp[:, -keep:, g], vp[:, -keep:, g]], axis=2))
            qp = _queries(xp, b_norm[j], w_q[j], q_norm[j], cos_p, sin_p)
            qs = _queries(xs, b_norm[j], w_q[j], q_norm[j], cos_s, sin_s)
            outs_p, lses_p, outs_s, lses_s = [], [], [], []
            for g in range(N_BRANCH):
                o, l = _dilated_prompt(qp[:, :, g], kp[:, :, g], vp[:, :, g], WINDOWS[g], DILATIONS[g])
                outs_p.append(o)
                lses_p.append(l)
                o, l = _dilated_sample(qs[:, :, g], kv_cat[g][:, :, 0], kv_cat[g][:, :, 1],
                                       caches[g].shape[1], WINDOWS[g], DILATIONS[g])
                outs_s.append(o)
                lses_s.append(l)
            xp = xp + _merge(outs_p, lses_p, w_o[j])
            xs = xs + _merge(outs_s, lses_s, w_o[j])
        xp = xp + _swiglu(_rmsnorm(xp, ffn_norm[layer]), w_gate[layer], w_up[layer], w_down[layer])
        xs = xs + _swiglu(_rmsnorm(xs, ffn_norm[layer]), w_gate[layer], w_up[layer], w_down[layer])

    pool_prompt = jnp.stack(pool_p, axis=1)
    pool_sample = jnp.stack(pool_s, axis=1)
    kv128_prompt = kv_new_p[0]
    kv512_prompt = kv_new_p[1]
    kv2048_prompt = kv_new_p[2]
    kv128_sample = kv_cat[0][:, -caches[0].shape[1]:]
    kv512_sample = kv_cat[1][:, -caches[1].shape[1]:]
    kv2048_sample = kv_cat[2][:, -caches[2].shape[1]:]
    return (xp, xs, pool_prompt, pool_sample, kv128_prompt, kv128_sample,
            kv512_prompt, kv512_sample, kv2048_prompt, kv2048_sample)
```

```python
import functools

import jax
import jax.numpy as jnp
import numpy as np
from jax import lax
from jax.experimental import pallas as pl
from jax.experimental.pallas import tpu as pltpu

F32 = jnp.float32
BF16 = jnp.bfloat16

D_MODEL = 1024
N_A_LAYERS = 2
N_B_LAYERS = 2
POOL_WINDOWS = (2, 4, 8, 16)
POOL_GROUP = 256
HALO = 16
POOL_HIST = 15
WINDOWS = (128, 512, 2048)
DILATIONS = (1, 4, 16)
N_BRANCH = 3
HEAD_DIM = 64
N_HEADS = 16
ATTN_WIDTH = N_HEADS * HEAD_DIM
D_FF = 2816
ROPE_THETA = 10000.0
EPS = 1e-6
PAST_LEN = 8192
BLK = 128
NEG = -1e30

LANES = 128
MXU_N = 256
VMEM_LIMIT = 56 * 1024 * 1024


def _cparams(n_axes):
    return pltpu.CompilerParams(dimension_semantics=("arbitrary",) * n_axes,
                                vmem_limit_bytes=VMEM_LIMIT)


def _rms(x, g):
    return x * lax.rsqrt(jnp.mean(x * x, axis=-1, keepdims=True) + EPS) * g


def _pool_kernel(x_ref, xh_ref, hist_ref, g_ref, w_ref, sc_ref, o_ref, ho_ref, *, tm, full_hist):
    j = pl.program_id(1)
    x = x_ref[0]
    g = g_ref[...]
    u = _rms(x, g)
    hu = _rms(xh_ref[0], g)
    h = jnp.where(j == 0, hist_ref[0], hu)
    ext = jnp.concatenate([h, u], axis=0)
    ho_ref[0] = ext[tm:tm + HALO]
    s = ext
    sums = {}
    for step in (1, 2, 4, 8):
        s = s + pltpu.roll(s, step, axis=0)
        sums[2 * step] = s
    row = j * tm + lax.broadcasted_iota(jnp.int32, (tm, 1), 0)
    for gi, win in enumerate(POOL_WINDOWS):
        lo, hi = gi * POOL_GROUP, (gi + 1) * POOL_GROUP
        sw = sums[win][HALO:, lo:hi]
        if full_hist:
            cnt = jnp.float32(win)
        else:
            cnt = jnp.minimum(row + 1, win).astype(F32)
        d = sw / cnt - u[:, lo:hi]
        y = jnp.dot(d.astype(BF16), w_ref[gi], preferred_element_type=F32)
        o_ref[0, :, lo:hi] = x[:, lo:hi] + y * sc_ref[:, lo:hi]


def _pool_layer(x, hist, a_norm, w_pool, scale, *, tm, full_hist):
    b, s, d = x.shape
    nt = s // tm
    if full_hist:
        halo_arr, halo_map = hist, (lambda bi, j: (bi, 0, 0))
    else:
        r = tm // HALO
        halo_arr, halo_map = x, (lambda bi, j: (bi, jnp.maximum(j * r - 1, 0), 0))
    return pl.pallas_call(
        functools.partial(_pool_kernel, tm=tm, full_hist=full_hist),
        grid=(b, nt),
        in_specs=[
            pl.BlockSpec((1, tm, d), lambda bi, j: (bi, j, 0)),
            pl.BlockSpec((1, HALO, d), halo_map),
            pl.BlockSpec((1, HALO, d), lambda bi, j: (bi, 0, 0)),
            pl.BlockSpec((1, d), lambda bi, j: (0, 0)),
            pl.BlockSpec((4, POOL_GROUP, POOL_GROUP), lambda bi, j: (0, 0, 0)),
            pl.BlockSpec((1, d), lambda bi, j: (0, 0)),
        ],
        out_specs=[
            pl.BlockSpec((1, tm, d), lambda bi, j: (bi, j, 0)),
            pl.BlockSpec((1, HALO, d), lambda bi, j: (bi, 0, 0)),
        ],
        out_shape=[jax.ShapeDtypeStruct((b, s, d), F32),
                   jax.ShapeDtypeStruct((b, HALO, d), F32)],
        compiler_params=_cparams(2),
        name="pool_layer",
    )(x, halo_arr, hist, a_norm, w_pool, scale)


FF_CHUNK = MXU_N


def _ffn_kernel(x_ref, g_ref, wg_ref, wu_ref, wd_ref, o_ref):
    x = x_ref[...]
    h = _rms(x, g_ref[...]).astype(BF16)
    acc = x
    for c in range(D_FF // FF_CHUNK):
        sl = slice(c * FF_CHUNK, (c + 1) * FF_CHUNK)
        a = jnp.dot(h, wg_ref[:, sl], preferred_element_type=F32)
        b = jnp.dot(h, wu_ref[:, sl], preferred_element_type=F32)
        t = (a * jax.nn.sigmoid(a) * b).astype(BF16)
        acc = acc + jnp.dot(t, wd_ref[sl, :], preferred_element_type=F32)
    o_ref[...] = acc


def _ffn(x, g, wg, wu, wd, layer, *, tm):
    n, d = x.shape
    g = g[layer:layer + 1]
    resident = dict(pipeline_mode=pl.Buffered(1))
    return pl.pallas_call(
        _ffn_kernel,
        grid=(n // tm,),
        in_specs=[
            pl.BlockSpec((tm, d), lambda i: (i, 0)),
            pl.BlockSpec((1, d), lambda i: (0, 0)),
            pl.BlockSpec((None, d, D_FF), lambda i: (layer, 0, 0), **resident),
            pl.BlockSpec((None, d, D_FF), lambda i: (layer, 0, 0), **resident),
            pl.BlockSpec((None, D_FF, d), lambda i: (layer, 0, 0), **resident),
        ],
        out_specs=pl.BlockSpec((tm, d), lambda i: (i, 0)),
        out_shape=jax.ShapeDtypeStruct((n, d), F32),
        compiler_params=_cparams(1),
        name="ffn",
    )(x, g, wg, wu, wd)


def _proj_kernel(x_ref, g_ref, w_ref, hg_ref, cos_ref, sa_ref, sb_ref, m_ref, *out_refs,
                 n_cols, n_rope, want_bf16, want_f32):
    h = _rms(x_ref[...], g_ref[...]).astype(BF16)
    cos, sa, sb = cos_ref[...], sa_ref[...], sb_ref[...]
    for c in range(n_cols // MXU_N):
        sl = slice(c * MXU_N, (c + 1) * MXU_N)
        y = jnp.dot(h, w_ref[:, sl], preferred_element_type=F32)
        if c * MXU_N < n_rope:
            msq = jnp.dot((y * y).astype(BF16), m_ref[...], preferred_element_type=F32)
            y = y * lax.rsqrt(msq + EPS) * hg_ref[:, sl]
            halves = []
            for p in range(MXU_N // LANES):
                z = y[:, p * LANES:(p + 1) * LANES]
                halves.append(z * cos + pltpu.roll(z, LANES - HEAD_DIM // 2, axis=1) * sa
                              + pltpu.roll(z, HEAD_DIM // 2, axis=1) * sb)
            y = jnp.concatenate(halves, axis=1)
        k = 0
        if want_bf16:
            out_refs[k][:, sl] = y.astype(BF16)
            k += 1
        if want_f32:
            out_refs[k][:, sl] = y


def _proj(x, g, w, w_index, hgain, tables, msum, *, tm, n_rope, want_bf16, f32_rows_per_seq, seq_len):
    n, d = x.shape
    n_cols = w.shape[-1]
    nt_seq = seq_len // tm
    cos, sa, sb = tables
    tab_map = lambda i: (i % nt_seq, 0)
    if w.ndim == 3:
        w_spec = pl.BlockSpec((None, d, n_cols), lambda i: (w_index, 0, 0), pipeline_mode=pl.Buffered(1))
    else:
        w_spec = pl.BlockSpec((d, n_cols), lambda i: (0, 0), pipeline_mode=pl.Buffered(1))
    out_specs, out_shape = [], []
    if want_bf16:
        out_specs.append(pl.BlockSpec((tm, n_cols), lambda i: (i, 0)))
        out_shape.append(jax.ShapeDtypeStruct((n, n_cols), BF16))
    want_f32 = f32_rows_per_seq > 0
    if want_f32:
        nt_tail = f32_rows_per_seq // tm
        j0 = nt_seq - nt_tail
        out_specs.append(pl.BlockSpec(
            (tm, n_cols), lambda i: ((i // nt_seq) * nt_tail + jnp.maximum(i % nt_seq - j0, 0), 0)))
        out_shape.append(jax.ShapeDtypeStruct((n // seq_len * f32_rows_per_seq, n_cols), F32))
    return pl.pallas_call(
        functools.partial(_proj_kernel, n_cols=n_cols, n_rope=n_rope, want_bf16=want_bf16, want_f32=want_f32),
        grid=(n // tm,),
        in_specs=[
            pl.BlockSpec((tm, d), lambda i: (i, 0)),
            pl.BlockSpec((1, d), lambda i: (0, 0)),
            w_spec,
            pl.BlockSpec((1, n_rope), lambda i: (0, 0)),
            pl.BlockSpec((tm, LANES), tab_map),
            pl.BlockSpec((tm, LANES), tab_map),
            pl.BlockSpec((tm, LANES), tab_map),
            pl.BlockSpec((MXU_N, MXU_N), lambda i: (0, 0)),
        ],
        out_specs=out_specs,
        out_shape=out_shape,
        compiler_params=_cparams(1),
        name="proj",
    )(x, g, w, hgain, cos, sa, sb, msum)


def _attn_kernel(q_ref, kp_ref, kc_ref, vp_ref, vc_ref, bias_ref, o_ref, lse_ref):
    bias = bias_ref[0]
    lane = lax.broadcasted_iota(jnp.int32, (BLK, LANES), 1)
    first = lane < HEAD_DIM
    for p in range(N_HEADS // 2):
        sl = slice(p * LANES, (p + 1) * LANES)
        q2 = q_ref[0, :, sl]
        kk = jnp.concatenate([kp_ref[0, :, sl], kc_ref[0, :, sl]], axis=0)
        vv = jnp.concatenate([vp_ref[0, :, sl], vc_ref[0, :, sl]], axis=0)
        outs, lses = [], []
        for half in range(2):
            qh = jnp.where(first if half == 0 else ~first, q2, jnp.zeros_like(q2))
            s = lax.dot_general(qh, kk, (((1,), (1,)), ((), ())), preferred_element_type=F32) + bias
            m = jnp.max(s, axis=-1, keepdims=True)
            e = jnp.exp(s - m)
            l = jnp.sum(e, axis=-1, keepdims=True)
            pv = jnp.dot(e.astype(BF16), vv, preferred_element_type=F32)
            outs.append(pv / l)
            lses.append(jnp.broadcast_to(m + jnp.log(l), (BLK, LANES)))
        o_ref[0, :, sl] = jnp.where(first, outs[0], outs[1]).astype(o_ref.dtype)
        lse_ref[0, :, sl] = jnp.where(first, lses[0], lses[1])


def _attn(q, k, v, bias):
    ns, l, w = q.shape
    cur = lambda s, j: (s, j, 0)
    prev = lambda s, j: (s, jnp.maximum(j - 1, 0), 0)
    blk = (1, BLK, w)
    return pl.pallas_call(
        _attn_kernel,
        grid=(ns, l // BLK),
        in_specs=[
            pl.BlockSpec(blk, cur),
            pl.BlockSpec(blk, prev), pl.BlockSpec(blk, cur),
            pl.BlockSpec(blk, prev), pl.BlockSpec(blk, cur),
            pl.BlockSpec((1, BLK, 2 * BLK), lambda s, j: (jnp.minimum(j, 1), 0, 0)),
        ],
        out_specs=[pl.BlockSpec(blk, cur), pl.BlockSpec(blk, cur)],
        out_shape=[jax.ShapeDtypeStruct((ns, l, w), BF16), jax.ShapeDtypeStruct((ns, l, w), F32)],
        compiler_params=_cparams(2),
        name="attn_prompt",
    )(q, k, k, v, v, bias)


HEADS_PER_STEP = 4


def _attn_sample_kernel(q_ref, cache_ref, new_ref, bias_c_ref, o_ref, lse_ref, *cache_out, hist, dil):
    n = pl.program_id(0)
    slot = (n % (LANES // 8)) * 8
    lane = lax.broadcasted_iota(jnp.int32, (8, LANES), 1)
    t_q = lax.broadcasted_iota(jnp.int32, (8, LANES), 0)
    diff = t_q - (lane - slot)
    ok = (diff >= 0) & (diff <= t_q) & ((diff & (dil - 1)) == 0)
    bias_n = jnp.where(ok, 0.0, NEG)
    bias_c = bias_c_ref[...]
    lane_c = lax.broadcasted_iota(jnp.int32, (HEAD_DIM, LANES), 1)
    nt = (((1,), (1,)), ((), ()))
    for h in range(HEADS_PER_STEP):
        q = q_ref[0, h]
        kt, vt = cache_ref[0, 0, h], cache_ref[0, 1, h]
        knt, vnt = new_ref[0, h], new_ref[1, h]
        s_c = jnp.dot(q, kt, preferred_element_type=F32) + bias_c
        s_n = jnp.dot(q, knt, preferred_element_type=F32) + bias_n
        m = jnp.maximum(jnp.max(s_c, axis=-1, keepdims=True), jnp.max(s_n, axis=-1, keepdims=True))
        e_c = jnp.exp(s_c - m)
        e_n = jnp.exp(s_n - m)
        l = jnp.sum(e_c, axis=-1, keepdims=True) + jnp.sum(e_n, axis=-1, keepdims=True)
        o = (lax.dot_general(e_c, vt, nt, preferred_element_type=F32)
             + lax.dot_general(e_n, vnt, nt, preferred_element_type=F32))
        o_ref[0, h] = o / l
        lse_ref[0, h] = jnp.broadcast_to(m + jnp.log(l), (8, HEAD_DIM))
        if cache_out:
            out_ref = cache_out[0]
            for kv, (old, new) in enumerate(((kt, knt), (vt, vnt))):
                shifted = pltpu.roll(old, hist - 8, axis=1)
                if hist > LANES:
                    out_ref[0, kv, h, :, :hist - LANES] = shifted[:, :hist - LANES]
                moved = pltpu.roll(new, (LANES - 8) - slot, axis=1)
                out_ref[0, kv, h, :, hist - LANES:] = jnp.where(
                    lane_c >= LANES - 8, moved, shifted[:, hist - LANES:])


def _attn_sample(q, cache, new_t, bias_c, branch, *, dil, write_cache):
    nreq = q.shape[1]
    hist = cache.shape[-1]
    hb = HEADS_PER_STEP
    o_spec = pl.BlockSpec((1, hb, 8, HEAD_DIM), lambda n, h: (n, h, 0, 0))
    out_specs = [o_spec, o_spec]
    out_shape = [jax.ShapeDtypeStruct((nreq, N_HEADS, 8, HEAD_DIM), F32)] * 2
    if write_cache:
        out_specs.append(pl.BlockSpec((1, 2, hb, HEAD_DIM, hist), lambda n, h: (n, 0, h, 0, 0)))
        out_shape.append(jax.ShapeDtypeStruct(cache.shape, F32))
    return pl.pallas_call(
        functools.partial(_attn_sample_kernel, hist=hist, dil=dil),
        grid=(nreq, N_HEADS // hb),
        in_specs=[
            pl.BlockSpec((None, 1, hb, 8, HEAD_DIM), lambda n, h: (branch, n, h, 0, 0)),
            pl.BlockSpec((1, 2, hb, HEAD_DIM, hist), lambda n, h: (n, 0, h, 0, 0)),
            pl.BlockSpec((2, None, hb, HEAD_DIM, LANES), lambda n, h: (0, branch, h, 0, n // (LANES // 8))),
            pl.BlockSpec((8, hist), lambda n, h: (0, 0)),
        ],
        out_specs=out_specs,
        out_shape=out_shape,
        compiler_params=_cparams(2),
        name="attn_sample",
    )(q, cache, new_t, bias_c)


def _merge_kernel(x_ref, o0, o1, o2, l0, l1, l2, w_ref, out_ref):
    ls = [l0[...], l1[...], l2[...]]
    m = jnp.maximum(jnp.maximum(ls[0], ls[1]), ls[2])
    es = [jnp.exp(l - m) for l in ls]
    den = es[0] + es[1] + es[2]
    num = (es[0] * o0[...].astype(F32) + es[1] * o1[...].astype(F32) + es[2] * o2[...].astype(F32))
    merged = (num / den).astype(BF16)
    out_ref[...] = x_ref[...] + jnp.dot(merged, w_ref[...], preferred_element_type=F32)


def _merge(x, outs, lses, w_o, layer, *, tm):
    n, d = x.shape
    tile = pl.BlockSpec((tm, d), lambda i: (i, 0))
    return pl.pallas_call(
        _merge_kernel,
        grid=(n // tm,),
        in_specs=[tile] * 7 + [pl.BlockSpec((None, d, d), lambda i: (layer, 0, 0))],
        out_specs=tile,
        out_shape=jax.ShapeDtypeStruct((n, d), F32),
        compiler_params=_cparams(1),
        name="merge_out",
    )(x, *outs, *lses, w_o)


def _rope_tables(pos):
    inv = jnp.power(ROPE_THETA, -jnp.arange(0, HEAD_DIM, 2, dtype=F32) / HEAD_DIM)
    ang = pos.astype(F32)[:, None] * inv[None, :]
    ang = jnp.concatenate([ang, ang], axis=-1)
    cos, sin = jnp.cos(ang), jnp.sin(ang)
    first = jnp.arange(HEAD_DIM) < HEAD_DIM // 2
    sa = jnp.where(first, -sin, 0.0)
    sb = jnp.where(first, 0.0, sin)
    tile2 = lambda a: jnp.concatenate([a, a], axis=-1)
    return tile2(cos), tile2(sa), tile2(sb)


def _head_mean_matrix():
    i = np.arange(MXU_N)
    return jnp.asarray((i[:, None] // HEAD_DIM == i[None, :] // HEAD_DIM) / HEAD_DIM, dtype=BF16)


def _prompt_bias():
    qi = np.arange(BLK)[:, None]
    ki = np.arange(2 * BLK)[None, :]
    rel = qi + BLK - ki
    band = (rel >= 0) & (rel <= BLK)
    first = band & (ki >= BLK)
    return jnp.asarray(np.where(np.stack([first, band]), 0.0, NEG), dtype=F32)


def _sample_bias(hist, dil):
    t = np.arange(8)[:, None]
    l = np.arange(hist)[None, :]
    ok_c = (l >= t) & ((l - t) % dil == 0)
    return jnp.asarray(np.where(ok_c, 0.0, NEG), dtype=F32)


def _to_classes(a, dil):
    b, s, w = a.shape
    return a.reshape(b, s // dil, dil, w).transpose(0, 2, 1, 3).reshape(b * dil, s // dil, w)


def _from_classes(a, dil, b):
    ns, l, w = a.shape
    return a.reshape(b, dil, l, w).transpose(0, 2, 1, 3).reshape(b, l * dil, w)


def kernel(x_prompt, x_sample, state_pool, cache_kv_w128, cache_kv_w512, cache_kv_w2048,
           a_norm, pool_w, pool_scale, kv_norm, w_kv, k_norm, b_norm, w_q, q_norm, w_o,
           ffn_norm, w_gate, w_up, w_down):
    bsz, seq, d = x_prompt.shape
    nreq, t_new, _ = x_sample.shape
    caches = (cache_kv_w128, cache_kv_w512, cache_kv_w2048)
    n_p, n_s = bsz * seq, nreq * t_new

    pool_w_b = pool_w.astype(BF16)
    w_kv_b, w_q_b, w_o_b = w_kv.astype(BF16), w_q.astype(BF16), w_o.astype(BF16)
    wg_b, wu_b, wd_b = w_gate.astype(BF16), w_up.astype(BF16), w_down.astype(BF16)

    tab_p = _rope_tables(jnp.arange(seq))
    tab_s = tuple(jnp.tile(a, (nreq, 1)) for a in _rope_tables(PAST_LEN + jnp.arange(t_new)))
    msum = _head_mean_matrix()
    k_gain = jnp.tile(k_norm[:, None, :], (1, N_HEADS, 1)).reshape(1, N_BRANCH * ATTN_WIDTH)
    bias_p = _prompt_bias()

    xp, xs = x_prompt, x_sample
    pool_p, pool_s = [], []
    for i in range(N_A_LAYERS):
        zeros_hist = jnp.zeros((bsz, HALO, d), F32)
        xp, hp = _pool_layer(xp, zeros_hist, a_norm[i:i + 1], pool_w_b[i], pool_scale[i:i + 1],
                             tm=512, full_hist=False)
        hist_s = jnp.pad(state_pool[:, i], ((0, 0), (1, 0), (0, 0)))
        xs, hs = _pool_layer(xs, hist_s, a_norm[i:i + 1], pool_w_b[i], pool_scale[i:i + 1],
                             tm=t_new, full_hist=True)
        pool_p.append(hp[:, 1:])
        pool_s.append(hs[:, 1:])
        xp = _ffn(xp.reshape(n_p, d), ffn_norm, wg_b, wu_b, wd_b, i, tm=512).reshape(bsz, seq, d)
        xs = _ffn(xs.reshape(n_s, d), ffn_norm, wg_b, wu_b, wd_b, i, tm=n_s).reshape(nreq, t_new, d)

    xp, xs = xp.reshape(n_p, d), xs.reshape(n_s, d)
    keep_max = max(min(w, seq) for w in WINDOWS)
    kv_norm2 = kv_norm.reshape(1, d)
    kv_p, kv_p_tail = _proj(xp, kv_norm2, w_kv_b, 0, k_gain, tab_p, msum, tm=256, n_rope=N_BRANCH * ATTN_WIDTH,
                            want_bf16=True, f32_rows_per_seq=keep_max, seq_len=seq)
    (kv_s,) = _proj(xs, kv_norm2, w_kv_b, 0, k_gain, tab_s, msum, tm=n_s, n_rope=N_BRANCH * ATTN_WIDTH,
                    want_bf16=False, f32_rows_per_seq=n_s, seq_len=n_s)

    tail = kv_p_tail.reshape(bsz, keep_max, 2, N_BRANCH, N_HEADS, HEAD_DIM)
    kv_new_p = [tail[:, keep_max - min(w, seq):, :, g] for g, w in enumerate(WINDOWS)]

    kv_p3 = kv_p.reshape(bsz, seq, 2 * N_BRANCH * ATTN_WIDTH)
    k_cls = [_to_classes(kv_p3[:, :, g * ATTN_WIDTH:(g + 1) * ATTN_WIDTH], dil)
             for g, dil in enumerate(DILATIONS)]
    v_cls = [_to_classes(kv_p3[:, :, (N_BRANCH + g) * ATTN_WIDTH:(N_BRANCH + g + 1) * ATTN_WIDTH], dil)
             for g, dil in enumerate(DILATIONS)]

    caches_t = [c.transpose(0, 2, 3, 4, 1) for c in caches]
    new_t = kv_s.T.reshape(2, N_BRANCH, N_HEADS, HEAD_DIM, n_s)
    sample_bias = [_sample_bias(c.shape[1], dil) for c, dil in zip(caches, DILATIONS)]

    kv_sample_out = [None] * N_BRANCH
    for j in range(N_B_LAYERS):
        layer = N_A_LAYERS + j
        q_gain = jnp.tile(q_norm[j][:, None, :], (1, N_HEADS, 1)).reshape(1, N_BRANCH * ATTN_WIDTH)
        q_gain = q_gain * (HEAD_DIM ** -0.5)
        bn = b_norm[j:j + 1]
        (q_p,) = _proj(xp, bn, w_q_b, j, q_gain, tab_p, msum, tm=512, n_rope=N_BRANCH * ATTN_WIDTH,
                       want_bf16=True, f32_rows_per_seq=0, seq_len=seq)
        (q_s,) = _proj(xs, bn, w_q_b, j, q_gain, tab_s, msum, tm=n_s, n_rope=N_BRANCH * ATTN_WIDTH,
                       want_bf16=False, f32_rows_per_seq=n_s, seq_len=n_s)
        q_p3 = q_p.reshape(bsz, seq, N_BRANCH * ATTN_WIDTH)
        q_s5 = q_s.reshape(nreq, t_new, N_BRANCH, N_HEADS, HEAD_DIM).transpose(2, 0, 3, 1, 4)

        outs_p, lses_p, outs_s, lses_s = [], [], [], []
        for g, dil in enumerate(DILATIONS):
            q_c = _to_classes(q_p3[:, :, g * ATTN_WIDTH:(g + 1) * ATTN_WIDTH], dil)
            o_c, lse_c = _attn(q_c, k_cls[g], v_cls[g], bias_p)
            outs_p.append(_from_classes(o_c, dil, bsz).reshape(n_p, ATTN_WIDTH))
            lses_p.append(_from_classes(lse_c, dil, bsz).reshape(n_p, ATTN_WIDTH))

            res = _attn_sample(q_s5, caches_t[g], new_t, sample_bias[g], g, dil=dil, write_cache=(j == 0))
            outs_s.append(res[0].transpose(0, 2, 1, 3).reshape(n_s, ATTN_WIDTH))
            lses_s.append(res[1].transpose(0, 2, 1, 3).reshape(n_s, ATTN_WIDTH))
            if j == 0:
                kv_sample_out[g] = res[2].transpose(0, 4, 1, 2, 3)

        xp = _merge(xp, outs_p, lses_p, w_o_b, j, tm=512)
        xs = _merge(xs, outs_s, lses_s, w_o_b, j, tm=n_s)
        xp = _ffn(xp, ffn_norm, wg_b, wu_b, wd_b, layer, tm=512)
        xs = _ffn(xs, ffn_norm, wg_b, wu_b, wd_b, layer, tm=n_s)

    return (xp.reshape(bsz, seq, d), xs.reshape(nreq, t_new, d),
            jnp.stack(pool_p, axis=1), jnp.stack(pool_s, axis=1),
            kv_new_p[0], kv_sample_out[0], kv_new_p[1], kv_sample_out[1],
            kv_new_p[2], kv_sample_out[2])
```

```python
import functools

import jax
import jax.numpy as jnp
import numpy as np
from jax import lax
from jax.experimental import pallas as pl
from jax.experimental.pallas import tpu as pltpu

F32 = jnp.float32
BF16 = jnp.bfloat16

D_MODEL = 1024
N_A_LAYERS = 2
N_B_LAYERS = 2
POOL_WINDOWS = (2, 4, 8, 16)
POOL_GROUP = 256
HALO = 16
POOL_HIST = 15
WINDOWS = (128, 512, 2048)
DILATIONS = (1, 4, 16)
N_BRANCH = 3
HEAD_DIM = 64
N_HEADS = 16
ATTN_WIDTH = N_HEADS * HEAD_DIM
D_FF = 2816
ROPE_THETA = 10000.0
EPS = 1e-6
PAST_LEN = 8192
BLK = 128
NEG = -1e30

LANES = 128
MXU_N = 256
VMEM_LIMIT = 56 * 1024 * 1024
TM_PROMPT = 512


def _cparams(n_axes):
    return pltpu.CompilerParams(dimension_semantics=("arbitrary",) * n_axes,
                                vmem_limit_bytes=VMEM_LIMIT)


def _rms(x, g):
    return x * lax.rsqrt(jnp.mean(x * x, axis=-1, keepdims=True) + EPS) * g


def _pool_kernel(x_ref, xh_ref, hist_ref, g_ref, w_ref, sc_ref, o_ref, ho_ref, *, tm, full_hist):
    j = pl.program_id(1)
    x = x_ref[0]
    g = g_ref[...]
    u = _rms(x, g)
    hu = _rms(xh_ref[0], g)
    h = jnp.where(j == 0, hist_ref[0], hu)
    ext = jnp.concatenate([h, u], axis=0)
    ho_ref[0] = ext[tm:tm + HALO]
    s = ext
    sums = {}
    for step in (1, 2, 4, 8):
        s = s + pltpu.roll(s, step, axis=0)
        sums[2 * step] = s
    row = j * tm + lax.broadcasted_iota(jnp.int32, (tm, 1), 0)
    for gi, win in enumerate(POOL_WINDOWS):
        lo, hi = gi * POOL_GROUP, (gi + 1) * POOL_GROUP
        sw = sums[win][HALO:, lo:hi]
        if full_hist:
            cnt = jnp.float32(win)
        else:
            cnt = jnp.minimum(row + 1, win).astype(F32)
        d = sw / cnt - u[:, lo:hi]
        y = jnp.dot(d.astype(BF16), w_ref[gi], preferred_element_type=F32)
        o_ref[0, :, lo:hi] = x[:, lo:hi] + y * sc_ref[:, lo:hi]


def _pool_layer(x, hist, a_norm, w_pool, scale, *, tm, full_hist):
    b, s, d = x.shape
    nt = s // tm
    if full_hist:
        halo_arr, halo_map = hist, (lambda bi, j: (bi, 0, 0))
    else:
        r = tm // HALO
        halo_arr, halo_map = x, (lambda bi, j: (bi, jnp.maximum(j * r - 1, 0), 0))
    return pl.pallas_call(
        functools.partial(_pool_kernel, tm=tm, full_hist=full_hist),
        grid=(b, nt),
        in_specs=[
            pl.BlockSpec((1, tm, d), lambda bi, j: (bi, j, 0)),
            pl.BlockSpec((1, HALO, d), halo_map),
            pl.BlockSpec((1, HALO, d), lambda bi, j: (bi, 0, 0)),
            pl.BlockSpec((1, d), lambda bi, j: (0, 0)),
            pl.BlockSpec((4, POOL_GROUP, POOL_GROUP), lambda bi, j: (0, 0, 0)),
            pl.BlockSpec((1, d), lambda bi, j: (0, 0)),
        ],
        out_specs=[
            pl.BlockSpec((1, tm, d), lambda bi, j: (bi, j, 0)),
            pl.BlockSpec((1, HALO, d), lambda bi, j: (bi, 0, 0)),
        ],
        out_shape=[jax.ShapeDtypeStruct((b, s, d), F32),
                   jax.ShapeDtypeStruct((b, HALO, d), F32)],
        compiler_params=_cparams(2),
        name="pool_layer",
    )(x, halo_arr, hist, a_norm, w_pool, scale)


FF_CHUNK = MXU_N


def _ffn_kernel(x_ref, g_ref, wg_ref, wu_ref, wd_ref, o_ref):
    x = x_ref[...]
    h = _rms(x, g_ref[...]).astype(BF16)
    acc = x
    for c in range(D_FF // FF_CHUNK):
        sl = slice(c * FF_CHUNK, (c + 1) * FF_CHUNK)
        a = jnp.dot(h, wg_ref[:, sl], preferred_element_type=F32)
        b = jnp.dot(h, wu_ref[:, sl], preferred_element_type=F32)
        t = (a * jax.nn.sigmoid(a) * b).astype(BF16)
        acc = acc + jnp.dot(t, wd_ref[sl, :], preferred_element_type=F32)
    o_ref[...] = acc


def _ffn(x, g, wg, wu, wd, layer, *, tm):
    n, d = x.shape
    g = g[layer:layer + 1]
    resident = dict(pipeline_mode=pl.Buffered(1))
    return pl.pallas_call(
        _ffn_kernel,
        grid=(n // tm,),
        in_specs=[
            pl.BlockSpec((tm, d), lambda i: (i, 0)),
            pl.BlockSpec((1, d), lambda i: (0, 0)),
            pl.BlockSpec((None, d, D_FF), lambda i: (layer, 0, 0), **resident),
            pl.BlockSpec((None, d, D_FF), lambda i: (layer, 0, 0), **resident),
            pl.BlockSpec((None, D_FF, d), lambda i: (layer, 0, 0), **resident),
        ],
        out_specs=pl.BlockSpec((tm, d), lambda i: (i, 0)),
        out_shape=jax.ShapeDtypeStruct((n, d), F32),
        compiler_params=_cparams(1),
        name="ffn",
    )(x, g, wg, wu, wd)


PROJ_COLS = N_BRANCH * ATTN_WIDTH


def _proj_chunk(h, w_ref, c, hg_ref, m_ref, cos, sa, sb, rope):
    sl = slice(c * MXU_N, (c + 1) * MXU_N)
    y = jnp.dot(h, w_ref[:, sl], preferred_element_type=F32)
    if rope:
        msq = jnp.dot((y * y).astype(BF16), m_ref[...], preferred_element_type=F32)
        y = y * lax.rsqrt(msq + EPS) * hg_ref[:, sl]
        halves = []
        for p in range(MXU_N // LANES):
            z = y[:, p * LANES:(p + 1) * LANES]
            halves.append(z * cos + pltpu.roll(z, LANES - HEAD_DIM // 2, axis=1) * sa
                          + pltpu.roll(z, HEAD_DIM // 2, axis=1) * sb)
        y = jnp.concatenate(halves, axis=1)
    return y


def _proj_sample_kernel(x_ref, g_ref, w_ref, hg_ref, cos_ref, sa_ref, sb_ref, m_ref, o_ref, *, n_rope):
    h = _rms(x_ref[...], g_ref[...]).astype(BF16)
    cos, sa, sb = cos_ref[...], sa_ref[...], sb_ref[...]
    for c in range(o_ref.shape[1] // MXU_N):
        o_ref[:, c * MXU_N:(c + 1) * MXU_N] = _proj_chunk(
            h, w_ref, c, hg_ref, m_ref, cos, sa, sb, c * MXU_N < n_rope)


def _proj_sample(x, g, w, w_index, hgain, tables, msum, *, n_rope):
    n, d = x.shape
    n_cols = w.shape[-1]
    full = lambda *shape: pl.BlockSpec(shape, lambda i: (0,) * len(shape))
    if w.ndim == 3:
        w_spec = pl.BlockSpec((None, d, n_cols), lambda i: (w_index, 0, 0))
    else:
        w_spec = full(d, n_cols)
    return pl.pallas_call(
        functools.partial(_proj_sample_kernel, n_rope=n_rope),
        grid=(1,),
        in_specs=[full(n, d), full(1, d), w_spec, full(1, n_rope),
                  full(n, LANES), full(n, LANES), full(n, LANES), full(MXU_N, MXU_N)],
        out_specs=full(n, n_cols),
        out_shape=jax.ShapeDtypeStruct((n, n_cols), F32),
        compiler_params=_cparams(1),
        name="proj_sample",
    )(x, g, w, hgain, *tables, msum)


def _proj_prompt_kernel(x_ref, g_ref, w_ref, hg_ref, cos_ref, sa_ref, sb_ref, m_ref, *refs,
                        tm, rope, tail_blocks, tail_start, n_carried):
    refs = refs[n_carried:]
    outs, scr = refs[:N_BRANCH], refs[-1]
    tails = refs[N_BRANCH:-1]
    j = pl.program_id(1)
    h = _rms(x_ref[...], g_ref[...]).astype(BF16)
    cos, sa, sb = cos_ref[...], sa_ref[...], sb_ref[...]
    per_branch = ATTN_WIDTH // MXU_N
    for c in range(PROJ_COLS // MXU_N):
        g, cc = divmod(c, per_branch)
        csl = slice(cc * MXU_N, (cc + 1) * MXU_N)
        y = _proj_chunk(h, w_ref, c, hg_ref, m_ref, cos, sa, sb, rope)
        dil = DILATIONS[g]
        if dil == 1:
            outs[g][0, 0, :, csl] = y.astype(BF16)
        else:
            for p in range(MXU_N // LANES):
                scr[p] = y[:, p * LANES:(p + 1) * LANES]
            for r in range(dil):
                for p in range(MXU_N // LANES):
                    lo = cc * MXU_N + p * LANES
                    outs[g][0, r, :, lo:lo + LANES] = scr[p, pl.ds(r, tm // dil, stride=dil), :].astype(BF16)
        if tails:
            blk = tail_blocks[g]

            @pl.when(j >= tail_start[g])
            def _(y=y, g=g, csl=csl, blk=blk):
                tails[g][0, 0, csl, :] = y[tm - blk:, :].T


def _proj_prompt(x, g, w, w_block, hgain, tables, msum, *, bsz, seq, tm, rope, tail_kv, tail_arrays):
    n, d = x.shape
    nt = seq // tm
    if w.ndim == 3:
        w_spec = pl.BlockSpec((None, d, PROJ_COLS), lambda b, j: (w_block, 0, 0), pipeline_mode=pl.Buffered(1))
    else:
        w_spec = pl.BlockSpec((d, PROJ_COLS), lambda b, j: (0, w_block), pipeline_mode=pl.Buffered(1))
    const = lambda *shape: pl.BlockSpec(shape, lambda b, j: (0,) * len(shape))
    tab = pl.BlockSpec((tm, LANES), lambda b, j: (j, 0))
    in_specs = [pl.BlockSpec((tm, d), lambda b, j: (b * nt + j, 0)), const(1, d), w_spec,
                const(1, PROJ_COLS), tab, tab, tab, const(MXU_N, MXU_N)]
    args = [x, g, w, hgain, *tables, msum]
    out_specs = [pl.BlockSpec((1, dil, tm // dil, ATTN_WIDTH), lambda b, j: (b, 0, j, 0)) for dil in DILATIONS]
    out_shape = [jax.ShapeDtypeStruct((bsz, dil, seq // dil, ATTN_WIDTH), BF16) for dil in DILATIONS]
    tail_blocks = tail_start = None
    aliases = {}
    if tail_kv is not None:
        keeps = [min(wd, seq) for wd in WINDOWS]
        tail_blocks = tuple(min(k, tm) for k in keeps)
        tail_start = tuple(nt - max(k // tm, 1) for k in keeps)
        for gi, (k, blk, j0) in enumerate(zip(keeps, tail_blocks, tail_start)):
            out_specs.append(pl.BlockSpec((1, 1, ATTN_WIDTH, blk),
                                          lambda b, j, j0=j0: (b, tail_kv, 0, jnp.maximum(j - j0, 0))))
            out_shape.append(jax.ShapeDtypeStruct((bsz, 2, ATTN_WIDTH, k), F32))
            if tail_arrays is not None:
                aliases[len(args)] = N_BRANCH + gi
                args.append(tail_arrays[gi])
                in_specs.append(pl.BlockSpec(memory_space=pl.ANY))
    kern = functools.partial(_proj_prompt_kernel, tm=tm, rope=rope, tail_blocks=tail_blocks,
                             tail_start=tail_start, n_carried=len(aliases))
    return pl.pallas_call(
        kern,
        grid=(bsz, nt),
        in_specs=in_specs,
        out_specs=out_specs,
        out_shape=out_shape,
        scratch_shapes=[pltpu.VMEM((MXU_N // LANES, tm, LANES), F32)],
        input_output_aliases=aliases,
        compiler_params=_cparams(2),
        name="proj_prompt",
    )(*args)


def _attn_kernel(q_ref, kp_ref, kc_ref, vp_ref, vc_ref, bias_ref, o_ref, lse_ref):
    bias = bias_ref[0]
    lane = lax.broadcasted_iota(jnp.int32, (BLK, LANES), 1)
    first = lane < HEAD_DIM
    lse_all = jnp.zeros((BLK, LANES), F32)
    for p in range(N_HEADS // 2):
        sl = slice(p * LANES, (p + 1) * LANES)
        q2 = q_ref[0, :, sl]
        kk = jnp.concatenate([kp_ref[0, :, sl], kc_ref[0, :, sl]], axis=0)
        vv = jnp.concatenate([vp_ref[0, :, sl], vc_ref[0, :, sl]], axis=0)
        outs = []
        for half in range(2):
            qh = jnp.where(first if half == 0 else ~first, q2, jnp.zeros_like(q2))
            s = lax.dot_general(qh, kk, (((1,), (1,)), ((), ())), preferred_element_type=F32) + bias
            m = jnp.max(s, axis=-1, keepdims=True)
            e = jnp.exp(s - m)
            l = jnp.sum(e, axis=-1, keepdims=True)
            pv = jnp.dot(e.astype(BF16), vv, preferred_element_type=F32)
            outs.append(pv / l)
            lse_all = jnp.where(lane == 2 * p + half, m + jnp.log(l), lse_all)
        o_ref[0, :, sl] = jnp.where(first, outs[0], outs[1]).astype(o_ref.dtype)
    lse_ref[0] = lse_all


def _attn(q, k, v, bias):
    ns, l, w = q.shape
    cur = lambda s, j: (s, j, 0)
    prev = lambda s, j: (s, jnp.maximum(j - 1, 0), 0)
    blk = (1, BLK, w)
    return pl.pallas_call(
        _attn_kernel,
        grid=(ns, l // BLK),
        in_specs=[
            pl.BlockSpec(blk, cur),
            pl.BlockSpec(blk, prev), pl.BlockSpec(blk, cur),
            pl.BlockSpec(blk, prev), pl.BlockSpec(blk, cur),
            pl.BlockSpec((1, BLK, 2 * BLK), lambda s, j: (jnp.minimum(j, 1), 0, 0)),
        ],
        out_specs=[pl.BlockSpec(blk, cur), pl.BlockSpec((1, BLK, LANES), cur)],
        out_shape=[jax.ShapeDtypeStruct((ns, l, w), BF16), jax.ShapeDtypeStruct((ns, l, LANES), F32)],
        compiler_params=_cparams(2),
        name="attn_prompt",
    )(q, k, k, v, v, bias)


SAMPLE_CACHE_BLOCK_POSITIONS = 8192


def _attn_sample_kernel(q_ref, cache_ref, new_ref, bias_c_ref, o_ref, lse_ref, *cache_out, hist, dil, hb):
    n = pl.program_id(0)
    slot = (n % (LANES // 8)) * 8
    lane = lax.broadcasted_iota(jnp.int32, (8, LANES), 1)
    t_q = lax.broadcasted_iota(jnp.int32, (8, LANES), 0)
    diff = t_q - (lane - slot)
    ok = (diff >= 0) & (diff <= t_q) & ((diff & (dil - 1)) == 0)
    bias_n = jnp.where(ok, 0.0, NEG)
    bias_c = bias_c_ref[...]
    lane_c = lax.broadcasted_iota(jnp.int32, (HEAD_DIM, LANES), 1)
    nt = (((1,), (1,)), ((), ()))
    for h in range(hb):
        q = q_ref[0, h]
        kt, vt = cache_ref[0, 0, h], cache_ref[0, 1, h]
        knt, vnt = new_ref[0, h], new_ref[1, h]
        s_c = jnp.dot(q, kt, preferred_element_type=F32) + bias_c
        s_n = jnp.dot(q, knt, preferred_element_type=F32) + bias_n
        m = jnp.maximum(jnp.max(s_c, axis=-1, keepdims=True), jnp.max(s_n, axis=-1, keepdims=True))
        e_c = jnp.exp(s_c - m)
        e_n = jnp.exp(s_n - m)
        l = jnp.sum(e_c, axis=-1, keepdims=True) + jnp.sum(e_n, axis=-1, keepdims=True)
        o = (lax.dot_general(e_c, vt, nt, preferred_element_type=F32)
             + lax.dot_general(e_n, vnt, nt, preferred_element_type=F32))
        o_ref[0, h] = o / l
        lse_ref[0, h] = jnp.broadcast_to(m + jnp.log(l), (8, HEAD_DIM))
        if cache_out:
            out_ref = cache_out[0]
            for kv, (old, new) in enumerate(((kt, knt), (vt, vnt))):
                shifted = pltpu.roll(old, hist - 8, axis=1)
                if hist > LANES:
                    out_ref[0, kv, h, :, :hist - LANES] = shifted[:, :hist - LANES]
                moved = pltpu.roll(new, (LANES - 8) - slot, axis=1)
                out_ref[0, kv, h, :, hist - LANES:] = jnp.where(
                    lane_c >= LANES - 8, moved, shifted[:, hist - LANES:])


def _attn_sample(q, cache, new_t, bias_c, branch, *, dil, write_cache):
    nreq = q.shape[1]
    hist = cache.shape[-1]
    hb = min(N_HEADS, SAMPLE_CACHE_BLOCK_POSITIONS // hist)
    o_spec = pl.BlockSpec((1, hb, 8, HEAD_DIM), lambda n, h: (n, h, 0, 0))
    out_specs = [o_spec, o_spec]
    out_shape = [jax.ShapeDtypeStruct((nreq, N_HEADS, 8, HEAD_DIM), F32)] * 2
    if write_cache:
        out_specs.append(pl.BlockSpec((1, 2, hb, HEAD_DIM, hist), lambda n, h: (n, 0, h, 0, 0)))
        out_shape.append(jax.ShapeDtypeStruct(cache.shape, F32))
    return pl.pallas_call(
        functools.partial(_attn_sample_kernel, hist=hist, dil=dil, hb=hb),
        grid=(nreq, N_HEADS // hb),
        in_specs=[
            pl.BlockSpec((None, 1, hb, 8, HEAD_DIM), lambda n, h: (branch, n, h, 0, 0)),
            pl.BlockSpec((1, 2, hb, HEAD_DIM, hist), lambda n, h: (n, 0, h, 0, 0)),
            pl.BlockSpec((2, None, hb, HEAD_DIM, LANES), lambda n, h: (0, branch, h, 0, n // (LANES // 8))),
            pl.BlockSpec((8, hist), lambda n, h: (0, 0)),
        ],
        out_specs=out_specs,
        out_shape=out_shape,
        compiler_params=_cparams(2),
        name="attn_sample",
    )(q, cache, new_t, bias_c)


def _merge_sample_kernel(x_ref, o0, o1, o2, l0, l1, l2, w_ref, out_ref):
    ls = [l0[...], l1[...], l2[...]]
    m = jnp.maximum(jnp.maximum(ls[0], ls[1]), ls[2])
    es = [jnp.exp(l - m) for l in ls]
    den = es[0] + es[1] + es[2]
    num = es[0] * o0[...] + es[1] * o1[...] + es[2] * o2[...]
    merged = (num / den).astype(BF16)
    out_ref[...] = x_ref[...] + jnp.dot(merged, w_ref[...], preferred_element_type=F32)


def _merge_sample(x, outs, lses, w_o, layer):
    n, d = x.shape
    tile = pl.BlockSpec((n, d), lambda i: (0, 0))
    return pl.pallas_call(
        _merge_sample_kernel,
        grid=(1,),
        in_specs=[tile] * 7 + [pl.BlockSpec((None, d, d), lambda i: (layer, 0, 0))],
        out_specs=tile,
        out_shape=jax.ShapeDtypeStruct((n, d), F32),
        compiler_params=_cparams(1),
        name="merge_sample",
    )(x, *outs, *lses, w_o)


WEIGHT_PART_LANES = N_HEADS


def _merge_prompt_kernel(x_ref, o0_ref, o1_ref, o2_ref, l0_ref, l1_ref, l2_ref, e_ref, w_ref, out_ref,
                         oscr1, oscr2, lscr1, lscr2, *, tm):
    o_refs, l_refs = (o0_ref, o1_ref, o2_ref), (l0_ref, l1_ref, l2_ref)
    o_scr, l_scr = (None, oscr1, oscr2), (None, lscr1, lscr2)
    os_, ls_ = [], []
    for g, dil in enumerate(DILATIONS):
        if dil == 1:
            os_.append(o_refs[g][0, 0].astype(F32))
            ls_.append(l_refs[g][0, 0])
        else:
            for r in range(dil):
                rows = pl.ds(r, tm // dil, stride=dil)
                o_r = o_refs[g][0, r].astype(F32)
                for c in range(ATTN_WIDTH // LANES):
                    o_scr[g][c, rows, :] = o_r[:, c * LANES:(c + 1) * LANES]
                l_scr[g][rows, :] = l_refs[g][0, r]
            os_.append(jnp.concatenate([o_scr[g][c] for c in range(ATTN_WIDTH // LANES)], axis=1))
            ls_.append(l_scr[g][...])
    m = jnp.maximum(jnp.maximum(ls_[0], ls_[1]), ls_[2])
    es = [jnp.exp(l - m) for l in ls_]
    den = es[0] + es[1] + es[2]
    head_lane = lax.broadcasted_iota(jnp.int32, (tm, LANES), 1) < N_HEADS
    packed = None
    for g in range(N_BRANCH):
        w = jnp.where(head_lane, es[g] / den, 0.0)
        hi = w.astype(BF16).astype(F32)
        lo = (w - hi).astype(BF16).astype(F32)
        for part, val in ((2 * g, hi), (2 * g + 1, lo)):
            if part:
                val = pltpu.roll(val, part * WEIGHT_PART_LANES, axis=1)
            packed = val if packed is None else packed + val
    spread = jnp.dot(packed.astype(BF16), e_ref[...], preferred_element_type=F32)
    merged = sum(spread[:, g * ATTN_WIDTH:(g + 1) * ATTN_WIDTH] * os_[g] for g in range(N_BRANCH))
    out_ref[...] = x_ref[...] + jnp.dot(merged.astype(BF16), w_ref[...], preferred_element_type=F32)


def _merge_prompt(x, outs, lses, spread_mat, w_o, layer, *, bsz, seq, tm):
    n, d = x.shape
    nt = seq // tm
    tile = pl.BlockSpec((tm, d), lambda b, j: (b * nt + j, 0))
    cls = lambda width: [pl.BlockSpec((1, dil, tm // dil, width), lambda b, j: (b, 0, j, 0))
                         for dil in DILATIONS]
    return pl.pallas_call(
        functools.partial(_merge_prompt_kernel, tm=tm),
        grid=(bsz, nt),
        in_specs=[tile, *cls(ATTN_WIDTH), *cls(LANES),
                  pl.BlockSpec((LANES, N_BRANCH * ATTN_WIDTH), lambda b, j: (0, 0)),
                  pl.BlockSpec((None, d, d), lambda b, j: (layer, 0, 0))],
        out_specs=tile,
        out_shape=jax.ShapeDtypeStruct((n, d), F32),
        scratch_shapes=[pltpu.VMEM((ATTN_WIDTH // LANES, tm, LANES), F32),
                        pltpu.VMEM((ATTN_WIDTH // LANES, tm, LANES), F32),
                        pltpu.VMEM((tm, LANES), F32), pltpu.VMEM((tm, LANES), F32)],
        compiler_params=_cparams(2),
        name="merge_prompt",
    )(x, *outs, *lses, spread_mat, w_o)


def _rope_tables(pos):
    inv = jnp.power(ROPE_THETA, -jnp.arange(0, HEAD_DIM, 2, dtype=F32) / HEAD_DIM)
    ang = pos.astype(F32)[:, None] * inv[None, :]
    ang = jnp.concatenate([ang, ang], axis=-1)
    cos, sin = jnp.cos(ang), jnp.sin(ang)
    first = jnp.arange(HEAD_DIM) < HEAD_DIM // 2
    sa = jnp.where(first, -sin, 0.0)
    sb = jnp.where(first, 0.0, sin)
    tile2 = lambda a: jnp.concatenate([a, a], axis=-1)
    return tile2(cos), tile2(sa), tile2(sb)


def _head_mean_matrix():
    i = np.arange(MXU_N)
    return jnp.asarray((i[:, None] // HEAD_DIM == i[None, :] // HEAD_DIM) / HEAD_DIM, dtype=BF16)


def _prompt_bias():
    qi = np.arange(BLK)[:, None]
    ki = np.arange(2 * BLK)[None, :]
    rel = qi + BLK - ki
    band = (rel >= 0) & (rel <= BLK)
    first = band & (ki >= BLK)
    return jnp.asarray(np.where(np.stack([first, band]), 0.0, NEG), dtype=F32)


def _sample_bias(hist, dil):
    t = np.arange(8)[:, None]
    l = np.arange(hist)[None, :]
    ok_c = (l >= t) & ((l - t) % dil == 0)
    return jnp.asarray(np.where(ok_c, 0.0, NEG), dtype=F32)


def _weight_spread_matrix():
    lane = np.arange(LANES)[:, None]
    col = np.arange(N_BRANCH * ATTN_WIDTH)[None, :]
    g, h = lane // (2 * WEIGHT_PART_LANES), lane % WEIGHT_PART_LANES
    hit = (lane < 2 * N_BRANCH * WEIGHT_PART_LANES) & (col // ATTN_WIDTH == g) & (col % ATTN_WIDTH // HEAD_DIM == h)
    return jnp.asarray(hit, dtype=BF16)


def kernel(x_prompt, x_sample, state_pool, cache_kv_w128, cache_kv_w512, cache_kv_w2048,
           a_norm, pool_w, pool_scale, kv_norm, w_kv, k_norm, b_norm, w_q, q_norm, w_o,
           ffn_norm, w_gate, w_up, w_down):
    bsz, seq, d = x_prompt.shape
    nreq, t_new, _ = x_sample.shape
    caches = (cache_kv_w128, cache_kv_w512, cache_kv_w2048)
    n_p, n_s = bsz * seq, nreq * t_new

    pool_w_b = pool_w.astype(BF16)
    w_kv_b, w_q_b, w_o_b = w_kv.astype(BF16), w_q.astype(BF16), w_o.astype(BF16)
    wg_b, wu_b, wd_b = w_gate.astype(BF16), w_up.astype(BF16), w_down.astype(BF16)

    tab_p = _rope_tables(jnp.arange(seq))
    tab_s = tuple(jnp.tile(a, (nreq, 1)) for a in _rope_tables(PAST_LEN + jnp.arange(t_new)))
    msum = _head_mean_matrix()
    k_gain = jnp.tile(k_norm[:, None, :], (1, N_HEADS, 1)).reshape(1, N_BRANCH * ATTN_WIDTH)
    bias_p = _prompt_bias()

    xp, xs = x_prompt, x_sample
    pool_p, pool_s = [], []
    for i in range(N_A_LAYERS):
        zeros_hist = jnp.zeros((bsz, HALO, d), F32)
        xp, hp = _pool_layer(xp, zeros_hist, a_norm[i:i + 1], pool_w_b[i], pool_scale[i:i + 1],
                             tm=TM_PROMPT, full_hist=False)
        hist_s = jnp.pad(state_pool[:, i], ((0, 0), (1, 0), (0, 0)))
        xs, hs = _pool_layer(xs, hist_s, a_norm[i:i + 1], pool_w_b[i], pool_scale[i:i + 1],
                             tm=t_new, full_hist=True)
        pool_p.append(hp[:, 1:])
        pool_s.append(hs[:, 1:])
        xp = _ffn(xp.reshape(n_p, d), ffn_norm, wg_b, wu_b, wd_b, i, tm=TM_PROMPT).reshape(bsz, seq, d)
        xs = _ffn(xs.reshape(n_s, d), ffn_norm, wg_b, wu_b, wd_b, i, tm=n_s).reshape(nreq, t_new, d)

    xp, xs = xp.reshape(n_p, d), xs.reshape(n_s, d)
    kv_norm2 = kv_norm.reshape(1, d)
    prompt = dict(bsz=bsz, seq=seq, tm=TM_PROMPT)
    *k_cls, kt0, kt1, kt2 = _proj_prompt(xp, kv_norm2, w_kv_b, 0, k_gain, tab_p, msum, rope=True,
                                         tail_kv=0, tail_arrays=None, **prompt)
    *v_cls, kvt0, kvt1, kvt2 = _proj_prompt(xp, kv_norm2, w_kv_b, 1, k_gain, tab_p, msum, rope=False,
                                            tail_kv=1, tail_arrays=(kt0, kt1, kt2), **prompt)
    kv_new_p = [t.reshape(bsz, 2, N_HEADS, HEAD_DIM, t.shape[-1]).transpose(0, 4, 1, 2, 3)
                for t in (kvt0, kvt1, kvt2)]
    k_cls = [a.reshape(bsz * dil, seq // dil, ATTN_WIDTH) for a, dil in zip(k_cls, DILATIONS)]
    v_cls = [a.reshape(bsz * dil, seq // dil, ATTN_WIDTH) for a, dil in zip(v_cls, DILATIONS)]

    kv_s = _proj_sample(xs, kv_norm2, w_kv_b, 0, k_gain, tab_s, msum, n_rope=PROJ_COLS)
    caches_t = [c.transpose(0, 2, 3, 4, 1) for c in caches]
    new_t = kv_s.T.reshape(2, N_BRANCH, N_HEADS, HEAD_DIM, n_s)
    sample_bias = [_sample_bias(c.shape[1], dil) for c, dil in zip(caches, DILATIONS)]
    spread_mat = _weight_spread_matrix()

    kv_sample_out = [None] * N_BRANCH
    for j in range(N_B_LAYERS):
        layer = N_A_LAYERS + j
        q_gain = jnp.tile(q_norm[j][:, None, :], (1, N_HEADS, 1)).reshape(1, PROJ_COLS)
        q_gain = q_gain * (HEAD_DIM ** -0.5)
        bn = b_norm[j:j + 1]
        q_cls = _proj_prompt(xp, bn, w_q_b, j, q_gain, tab_p, msum, rope=True,
                             tail_kv=None, tail_arrays=None, **prompt)
        q_s = _proj_sample(xs, bn, w_q_b, j, q_gain, tab_s, msum, n_rope=PROJ_COLS)
        q_s5 = q_s.reshape(nreq, t_new, N_BRANCH, N_HEADS, HEAD_DIM).transpose(2, 0, 3, 1, 4)

        outs_p, lses_p, outs_s, lses_s = [], [], [], []
        for g, dil in enumerate(DILATIONS):
            q_c = q_cls[g].reshape(bsz * dil, seq // dil, ATTN_WIDTH)
            o_c, lse_c = _attn(q_c, k_cls[g], v_cls[g], bias_p)
            outs_p.append(o_c.reshape(bsz, dil, seq // dil, ATTN_WIDTH))
            lses_p.append(lse_c.reshape(bsz, dil, seq // dil, LANES))

            res = _attn_sample(q_s5, caches_t[g], new_t, sample_bias[g], g, dil=dil, write_cache=(j == 0))
            outs_s.append(res[0].transpose(0, 2, 1, 3).reshape(n_s, ATTN_WIDTH))
            lses_s.append(res[1].transpose(0, 2, 1, 3).reshape(n_s, ATTN_WIDTH))
            if j == 0:
                kv_sample_out[g] = res[2].transpose(0, 4, 1, 2, 3)

        xp = _merge_prompt(xp, outs_p, lses_p, spread_mat, w_o_b, j, **prompt)
        xs = _merge_sample(xs, outs_s, lses_s, w_o_b, j)
        xp = _ffn(xp, ffn_norm, wg_b, wu_b, wd_b, layer, tm=TM_PROMPT)
        xs = _ffn(xs, ffn_norm, wg_b, wu_b, wd_b, layer, tm=n_s)

    return (xp.reshape(bsz, seq, d), xs.reshape(nreq, t_new, d),
            jnp.stack(pool_p, axis=1), jnp.stack(pool_s, axis=1),
            kv_new_p[0], kv_sample_out[0], kv_new_p[1], kv_sample_out[1],
            kv_new_p[2], kv_sample_out[2])
```

```python
import functools

import jax
import jax.numpy as jnp
import numpy as np
from jax import lax
from jax.experimental import pallas as pl
from jax.experimental.pallas import tpu as pltpu

F32 = jnp.float32
BF16 = jnp.bfloat16

D_MODEL = 1024
N_A_LAYERS = 2
N_B_LAYERS = 2
POOL_WINDOWS = (2, 4, 8, 16)
POOL_GROUP = 256
HALO = 16
POOL_HIST = 15
WINDOWS = (128, 512, 2048)
DILATIONS = (1, 4, 16)
N_BRANCH = 3
HEAD_DIM = 64
N_HEADS = 16
ATTN_WIDTH = N_HEADS * HEAD_DIM
D_FF = 2816
ROPE_THETA = 10000.0
EPS = 1e-6
PAST_LEN = 8192
BLK = 128
NEG = -1e30

LANES = 128
MXU_N = 256
VMEM_LIMIT = 56 * 1024 * 1024
TM_PROMPT = 512


def _cparams(n_axes):
    return pltpu.CompilerParams(dimension_semantics=("arbitrary",) * n_axes,
                                vmem_limit_bytes=VMEM_LIMIT)


def _rms(x, g):
    return x * lax.rsqrt(jnp.mean(x * x, axis=-1, keepdims=True) + EPS) * g


def _pool_kernel(x_ref, xh_ref, hist_ref, g_ref, w_ref, sc_ref, o_ref, ho_ref, *, tm, full_hist):
    j = pl.program_id(1)
    nb = x_ref.shape[0]
    x = x_ref[...]
    g = g_ref[...]
    u = _rms(x, g)
    hu = _rms(xh_ref[...], g)
    h = jnp.where(j == 0, hist_ref[...], hu)
    ext = jnp.concatenate([h, u], axis=1)
    ho_ref[...] = ext[:, tm:tm + HALO]
    s = ext
    sums = {}
    for step in (1, 2, 4, 8):
        s = s + pltpu.roll(s, step, axis=1)
        sums[2 * step] = s
    row = j * tm + lax.broadcasted_iota(jnp.int32, (1, tm, 1), 1)
    for gi, win in enumerate(POOL_WINDOWS):
        lo, hi = gi * POOL_GROUP, (gi + 1) * POOL_GROUP
        sw = sums[win][:, HALO:, lo:hi]
        if full_hist:
            cnt = jnp.float32(win)
        else:
            cnt = jnp.minimum(row + 1, win).astype(F32)
        d = (sw / cnt - u[:, :, lo:hi]).reshape(nb * tm, POOL_GROUP)
        y = jnp.dot(d.astype(BF16), w_ref[gi], preferred_element_type=F32).reshape(nb, tm, POOL_GROUP)
        o_ref[:, :, lo:hi] = x[:, :, lo:hi] + y * sc_ref[:, lo:hi]


def _pool_layer(x, hist, a_norm, w_pool, scale, *, nb, tm, full_hist):
    b, s, d = x.shape
    nt = s // tm
    if full_hist:
        halo_arr, halo_map = hist, (lambda bi, j: (bi, 0, 0))
    else:
        r = tm // HALO
        halo_arr, halo_map = x, (lambda bi, j: (bi, jnp.maximum(j * r - 1, 0), 0))
    return pl.pallas_call(
        functools.partial(_pool_kernel, tm=tm, full_hist=full_hist),
        grid=(b // nb, nt),
        in_specs=[
            pl.BlockSpec((nb, tm, d), lambda bi, j: (bi, j, 0)),
            pl.BlockSpec((nb, HALO, d), halo_map),
            pl.BlockSpec((nb, HALO, d), lambda bi, j: (bi, 0, 0)),
            pl.BlockSpec((1, d), lambda bi, j: (0, 0)),
            pl.BlockSpec((4, POOL_GROUP, POOL_GROUP), lambda bi, j: (0, 0, 0)),
            pl.BlockSpec((1, d), lambda bi, j: (0, 0)),
        ],
        out_specs=[
            pl.BlockSpec((nb, tm, d), lambda bi, j: (bi, j, 0)),
            pl.BlockSpec((nb, HALO, d), lambda bi, j: (bi, 0, 0)),
        ],
        out_shape=[jax.ShapeDtypeStruct((b, s, d), F32),
                   jax.ShapeDtypeStruct((b, HALO, d), F32)],
        compiler_params=_cparams(2),
        name="pool_layer",
    )(x, halo_arr, hist, a_norm, w_pool, scale)


FF_CHUNK = MXU_N


def _ffn_kernel(x_ref, g_ref, wg_ref, wu_ref, wd_ref, o_ref):
    x = x_ref[...]
    h = _rms(x, g_ref[...]).astype(BF16)
    acc = x
    for c in range(D_FF // FF_CHUNK):
        sl = slice(c * FF_CHUNK, (c + 1) * FF_CHUNK)
        a = jnp.dot(h, wg_ref[:, sl], preferred_element_type=F32)
        b = jnp.dot(h, wu_ref[:, sl], preferred_element_type=F32)
        t = (a * jax.nn.sigmoid(a) * b).astype(BF16)
        acc = acc + jnp.dot(t, wd_ref[sl, :], preferred_element_type=F32)
    o_ref[...] = acc


def _ffn(x, g, wg, wu, wd, layer, *, tm):
    n, d = x.shape
    g = g[layer:layer + 1]
    resident = dict(pipeline_mode=pl.Buffered(1))
    return pl.pallas_call(
        _ffn_kernel,
        grid=(n // tm,),
        in_specs=[
            pl.BlockSpec((tm, d), lambda i: (i, 0)),
            pl.BlockSpec((1, d), lambda i: (0, 0)),
            pl.BlockSpec((None, d, D_FF), lambda i: (layer, 0, 0), **resident),
            pl.BlockSpec((None, d, D_FF), lambda i: (layer, 0, 0), **resident),
            pl.BlockSpec((None, D_FF, d), lambda i: (layer, 0, 0), **resident),
        ],
        out_specs=pl.BlockSpec((tm, d), lambda i: (i, 0)),
        out_shape=jax.ShapeDtypeStruct((n, d), F32),
        compiler_params=_cparams(1),
        name="ffn",
    )(x, g, wg, wu, wd)


PROJ_COLS = N_BRANCH * ATTN_WIDTH


def _proj_dot(h, w_ref, c):
    return jnp.dot(h, w_ref[:, c * MXU_N:(c + 1) * MXU_N], preferred_element_type=F32)


def _head_norm_rope(y, c, hg_ref, m_ref, cos, sa, sb):
    msq = jnp.dot((y * y).astype(BF16), m_ref[...], preferred_element_type=F32)
    y = y * lax.rsqrt(msq + EPS) * hg_ref[:, c * MXU_N:(c + 1) * MXU_N]
    halves = []
    for p in range(MXU_N // LANES):
        z = y[:, p * LANES:(p + 1) * LANES]
        halves.append(z * cos + pltpu.roll(z, LANES - HEAD_DIM // 2, axis=1) * sa
                      + pltpu.roll(z, HEAD_DIM // 2, axis=1) * sb)
    return jnp.concatenate(halves, axis=1)


def _proj_chunk(h, w_ref, c, hg_ref, m_ref, cos, sa, sb, rope):
    y = _proj_dot(h, w_ref, c)
    return _head_norm_rope(y, c, hg_ref, m_ref, cos, sa, sb) if rope else y


def _proj_sample_kernel(x_ref, g_ref, w_ref, hg_ref, cos_ref, sa_ref, sb_ref, m_ref, o_ref, *, n_rope):
    h = _rms(x_ref[...], g_ref[...]).astype(BF16)
    cos, sa, sb = cos_ref[...], sa_ref[...], sb_ref[...]
    for c in range(o_ref.shape[1] // MXU_N):
        o_ref[:, c * MXU_N:(c + 1) * MXU_N] = _proj_chunk(
            h, w_ref, c, hg_ref, m_ref, cos, sa, sb, c * MXU_N < n_rope)


def _proj_sample(x, g, w, w_index, hgain, tables, msum, *, n_rope):
    n, d = x.shape
    n_cols = w.shape[-1]
    full = lambda *shape: pl.BlockSpec(shape, lambda i: (0,) * len(shape))
    if w.ndim == 3:
        w_spec = pl.BlockSpec((None, d, n_cols), lambda i: (w_index, 0, 0))
    else:
        w_spec = full(d, n_cols)
    return pl.pallas_call(
        functools.partial(_proj_sample_kernel, n_rope=n_rope),
        grid=(1,),
        in_specs=[full(n, d), full(1, d), w_spec, full(1, n_rope),
                  full(n, LANES), full(n, LANES), full(n, LANES), full(MXU_N, MXU_N)],
        out_specs=full(n, n_cols),
        out_shape=jax.ShapeDtypeStruct((n, n_cols), F32),
        compiler_params=_cparams(1),
        name="proj_sample",
    )(x, g, w, hgain, *tables, msum)


def _proj_prompt_kernel(x_ref, g_ref, w_ref, hg_ref, cos_ref, sa_ref, sb_ref, m_ref, *refs,
                        tm, rope, tail_blocks, tail_start, n_carried):
    refs = refs[n_carried:]
    outs, yscr = refs[:N_BRANCH], refs[-1]
    tails = refs[N_BRANCH:-1]
    j = pl.program_id(1)
    h = _rms(x_ref[...], g_ref[...]).astype(BF16)
    cos, sa, sb = cos_ref[...], sa_ref[...], sb_ref[...]
    per_branch = ATTN_WIDTH // MXU_N
    slabs = MXU_N // LANES
    n_chunks = PROJ_COLS // MXU_N
    y_next = _proj_dot(h, w_ref, 0)
    for c in range(n_chunks):
        g, cc = divmod(c, per_branch)
        y = y_next
        if c + 1 < n_chunks:
            y_next = _proj_dot(h, w_ref, c + 1)
        if rope:
            y = _head_norm_rope(y, c, hg_ref, m_ref, cos, sa, sb)
        dil = DILATIONS[g]
        if dil == 1:
            outs[g][0, 0, :, cc * MXU_N:(cc + 1) * MXU_N] = y.astype(BF16)
        if dil > 1 or tails:
            for p in range(slabs):
                yscr[c * slabs + p] = y[:, p * LANES:(p + 1) * LANES]
        if dil > 1:
            for r in range(dil):
                for p in range(slabs):
                    lo = cc * MXU_N + p * LANES
                    outs[g][0, r, :, lo:lo + LANES] = (
                        yscr[c * slabs + p, pl.ds(r, tm // dil, stride=dil), :].astype(BF16))
    for g in range(N_BRANCH if tails else 0):
        blk = tail_blocks[g]

        @pl.when(j >= tail_start[g])
        def _(g=g, blk=blk):
            for s in range(ATTN_WIDTH // LANES):
                slab = yscr[g * (ATTN_WIDTH // LANES) + s, tm - blk:, :]
                tails[g][0, 0, s * LANES:(s + 1) * LANES, :] = slab.T


def _proj_prompt(x, g, w, w_block, hgain, tables, msum, *, bsz, seq, tm, rope, tail_kv, tail_arrays):
    n, d = x.shape
    nt = seq // tm
    if w.ndim == 3:
        w_spec = pl.BlockSpec((None, d, PROJ_COLS), lambda b, j: (w_block, 0, 0), pipeline_mode=pl.Buffered(1))
    else:
        w_spec = pl.BlockSpec((d, PROJ_COLS), lambda b, j: (0, w_block), pipeline_mode=pl.Buffered(1))
    const = lambda *shape: pl.BlockSpec(shape, lambda b, j: (0,) * len(shape))
    tab = pl.BlockSpec((tm, LANES), lambda b, j: (j, 0))
    in_specs = [pl.BlockSpec((tm, d), lambda b, j: (b * nt + j, 0)), const(1, d), w_spec,
                const(1, PROJ_COLS), tab, tab, tab, const(MXU_N, MXU_N)]
    args = [x, g, w, hgain, *tables, msum]
    out_specs = [pl.BlockSpec((1, dil, tm // dil, ATTN_WIDTH), lambda b, j: (b, 0, j, 0)) for dil in DILATIONS]
    out_shape = [jax.ShapeDtypeStruct((bsz, dil, seq // dil, ATTN_WIDTH), BF16) for dil in DILATIONS]
    tail_blocks = tail_start = None
    aliases = {}
    if tail_kv is not None:
        keeps = [min(wd, seq) for wd in WINDOWS]
        tail_blocks = tuple(min(k, tm) for k in keeps)
        tail_start = tuple(nt - max(k // tm, 1) for k in keeps)
        for gi, (k, blk, j0) in enumerate(zip(keeps, tail_blocks, tail_start)):
            out_specs.append(pl.BlockSpec((1, 1, ATTN_WIDTH, blk),
                                          lambda b, j, j0=j0: (b, tail_kv, 0, jnp.maximum(j - j0, 0))))
            out_shape.append(jax.ShapeDtypeStruct((bsz, 2, ATTN_WIDTH, k), F32))
            if tail_arrays is not None:
                aliases[len(args)] = N_BRANCH + gi
                args.append(tail_arrays[gi])
                in_specs.append(pl.BlockSpec(memory_space=pl.ANY))
    kern = functools.partial(_proj_prompt_kernel, tm=tm, rope=rope, tail_blocks=tail_blocks,
                             tail_start=tail_start, n_carried=len(aliases))
    return pl.pallas_call(
        kern,
        grid=(bsz, nt),
        in_specs=in_specs,
        out_specs=out_specs,
        out_shape=out_shape,
        scratch_shapes=[pltpu.VMEM((PROJ_COLS // LANES, tm, LANES), F32)],
        input_output_aliases=aliases,
        compiler_params=_cparams(2),
        name="proj_prompt",
    )(*args)


ATTN_Q_BLOCKS = 2
LN2 = float(np.log(2.0))
LOG2E = float(1.0 / np.log(2.0))


def _lse_lane(head):
    return head if head % 2 == 0 else HEAD_DIM + head


def _attn_kernel(q_ref, kp_ref, kc_ref, vp_ref, vc_ref, bias_ref, o_ref, lse_ref):
    j = pl.program_id(1)
    lane = lax.broadcasted_iota(jnp.int32, (BLK, LANES), 1)
    first = lane < HEAD_DIM
    nt = (((1,), (1,)), ((), ()))
    bias_first = bias_ref[jnp.minimum(j, 1)]

    def keys_values(b, p, ref_prev, ref_cur):
        sl = slice(p * LANES, (p + 1) * LANES)
        prev = ref_prev[0, :, sl] if b == 0 else ref_cur[0, (b - 1) * BLK:b * BLK, sl]
        return jnp.concatenate([prev, ref_cur[0, b * BLK:(b + 1) * BLK, sl]], axis=0)

    def scores(b, p):
        q2 = q_ref[0, b * BLK:(b + 1) * BLK, p * LANES:(p + 1) * LANES]
        zero = jnp.zeros_like(q2)
        qq = jnp.concatenate([jnp.where(first, q2, zero), jnp.where(first, zero, q2)], axis=0)
        kk = keys_values(b, p, kp_ref, kc_ref)
        return lax.dot_general(qq, kk, nt, preferred_element_type=F32) + (bias_first if b == 0 else bias_ref[1])

    pairs = [(b, p) for b in range(ATTN_Q_BLOCKS) for p in range(N_HEADS // 2)]
    s_next = scores(*pairs[0])
    lse_all = None
    for i, (b, p) in enumerate(pairs):
        s = s_next
        if i + 1 < len(pairs):
            s_next = scores(*pairs[i + 1])
        m = jnp.max(s, axis=-1, keepdims=True)
        e = jnp.exp2(s - m).astype(BF16)
        vv = keys_values(b, p, vp_ref, vc_ref)
        v1 = jnp.concatenate([vv, jnp.ones_like(vv)], axis=1)
        pv = jnp.dot(e, v1, preferred_element_type=F32)
        rows, sl = slice(b * BLK, (b + 1) * BLK), slice(p * LANES, (p + 1) * LANES)
        num = jnp.where(first, pv[:BLK, :LANES], pv[BLK:, :LANES])
        den = jnp.where(first, pv[:BLK, LANES:], pv[BLK:, LANES:])
        o_ref[0, rows, sl] = (num / den).astype(o_ref.dtype)
        lse_pair = (jnp.where(first, m[:BLK], m[BLK:]) + jnp.log2(den)) * LN2
        here = (lane == _lse_lane(2 * p)) | (lane == _lse_lane(2 * p + 1))
        lse_all = jnp.where(here, lse_pair, jnp.zeros_like(lse_pair) if p == 0 else lse_all)
        if p == N_HEADS // 2 - 1:
            lse_ref[0, rows, :] = lse_all


def _attn(q, k, v, bias):
    ns, l, w = q.shape
    nq = ATTN_Q_BLOCKS
    cur = lambda s, j: (s, j, 0)
    prev = lambda s, j: (s, jnp.maximum(j * nq - 1, 0), 0)
    big, small = (1, nq * BLK, w), (1, BLK, w)
    return pl.pallas_call(
        _attn_kernel,
        grid=(ns, l // (nq * BLK)),
        in_specs=[
            pl.BlockSpec(big, cur),
            pl.BlockSpec(small, prev), pl.BlockSpec(big, cur),
            pl.BlockSpec(small, prev), pl.BlockSpec(big, cur),
            pl.BlockSpec((2, 2 * BLK, 2 * BLK), lambda s, j: (0, 0, 0)),
        ],
        out_specs=[pl.BlockSpec(big, cur), pl.BlockSpec((1, nq * BLK, LANES), cur)],
        out_shape=[jax.ShapeDtypeStruct((ns, l, w), BF16), jax.ShapeDtypeStruct((ns, l, LANES), F32)],
        compiler_params=_cparams(2),
        name="attn_prompt",
    )(q, k, k, v, v, bias)


SAMPLE_CACHE_BLOCK_POSITIONS = 8192


def _attn_sample_kernel(q_ref, cache_ref, new_ref, bias_c_ref, o_ref, lse_ref, *cache_out, hist, dil, hb):
    n = pl.program_id(0)
    slot = (n % (LANES // 8)) * 8
    lane = lax.broadcasted_iota(jnp.int32, (8, LANES), 1)
    t_q = lax.broadcasted_iota(jnp.int32, (8, LANES), 0)
    diff = t_q - (lane - slot)
    ok = (diff >= 0) & (diff <= t_q) & ((diff & (dil - 1)) == 0)
    bias_n = jnp.where(ok, 0.0, NEG)
    bias_c = bias_c_ref[...]
    lane_c = lax.broadcasted_iota(jnp.int32, (HEAD_DIM, LANES), 1)
    nt = (((1,), (1,)), ((), ()))
    for h in range(hb):
        q = q_ref[0, h]
        kt, vt = cache_ref[0, 0, h], cache_ref[0, 1, h]
        knt, vnt = new_ref[0, h], new_ref[1, h]
        s_c = jnp.dot(q, kt, preferred_element_type=F32) + bias_c
        s_n = jnp.dot(q, knt, preferred_element_type=F32) + bias_n
        m = jnp.maximum(jnp.max(s_c, axis=-1, keepdims=True), jnp.max(s_n, axis=-1, keepdims=True))
        e_c = jnp.exp(s_c - m)
        e_n = jnp.exp(s_n - m)
        l = jnp.sum(e_c, axis=-1, keepdims=True) + jnp.sum(e_n, axis=-1, keepdims=True)
        o = (lax.dot_general(e_c, vt, nt, preferred_element_type=F32)
             + lax.dot_general(e_n, vnt, nt, preferred_element_type=F32))
        o_ref[0, h] = o / l
        lse_ref[0, h] = jnp.broadcast_to(m + jnp.log(l), (8, HEAD_DIM))
        if cache_out:
            out_ref = cache_out[0]
            for kv, (old, new) in enumerate(((kt, knt), (vt, vnt))):
                shifted = pltpu.roll(old, hist - 8, axis=1)
                if hist > LANES:
                    out_ref[0, kv, h, :, :hist - LANES] = shifted[:, :hist - LANES]
                moved = pltpu.roll(new, (LANES - 8) - slot, axis=1)
                out_ref[0, kv, h, :, hist - LANES:] = jnp.where(
                    lane_c >= LANES - 8, moved, shifted[:, hist - LANES:])


def _attn_sample(q, cache, new_t, bias_c, branch, *, dil, write_cache):
    nreq = q.shape[1]
    hist = cache.shape[-1]
    hb = min(N_HEADS, SAMPLE_CACHE_BLOCK_POSITIONS // hist)
    o_spec = pl.BlockSpec((1, hb, 8, HEAD_DIM), lambda n, h: (n, h, 0, 0))
    out_specs = [o_spec, o_spec]
    out_shape = [jax.ShapeDtypeStruct((nreq, N_HEADS, 8, HEAD_DIM), F32)] * 2
    if write_cache:
        out_specs.append(pl.BlockSpec((1, 2, hb, HEAD_DIM, hist), lambda n, h: (n, 0, h, 0, 0)))
        out_shape.append(jax.ShapeDtypeStruct(cache.shape, F32))
    return pl.pallas_call(
        functools.partial(_attn_sample_kernel, hist=hist, dil=dil, hb=hb),
        grid=(nreq, N_HEADS // hb),
        in_specs=[
            pl.BlockSpec((None, 1, hb, 8, HEAD_DIM), lambda n, h: (branch, n, h, 0, 0)),
            pl.BlockSpec((1, 2, hb, HEAD_DIM, hist), lambda n, h: (n, 0, h, 0, 0)),
            pl.BlockSpec((2, None, hb, HEAD_DIM, LANES), lambda n, h: (0, branch, h, 0, n // (LANES // 8))),
            pl.BlockSpec((8, hist), lambda n, h: (0, 0)),
        ],
        out_specs=out_specs,
        out_shape=out_shape,
        compiler_params=_cparams(2),
        name="attn_sample",
    )(q, cache, new_t, bias_c)


def _merge_sample_kernel(x_ref, o0, o1, o2, l0, l1, l2, w_ref, out_ref):
    ls = [l0[...], l1[...], l2[...]]
    m = jnp.maximum(jnp.maximum(ls[0], ls[1]), ls[2])
    es = [jnp.exp(l - m) for l in ls]
    den = es[0] + es[1] + es[2]
    num = es[0] * o0[...] + es[1] * o1[...] + es[2] * o2[...]
    merged = (num / den).astype(BF16)
    out_ref[...] = x_ref[...] + jnp.dot(merged, w_ref[...], preferred_element_type=F32)


def _merge_sample(x, outs, lses, w_o, layer):
    n, d = x.shape
    tile = pl.BlockSpec((n, d), lambda i: (0, 0))
    return pl.pallas_call(
        _merge_sample_kernel,
        grid=(1,),
        in_specs=[tile] * 7 + [pl.BlockSpec((None, d, d), lambda i: (layer, 0, 0))],
        out_specs=tile,
        out_shape=jax.ShapeDtypeStruct((n, d), F32),
        compiler_params=_cparams(1),
        name="merge_sample",
    )(x, *outs, *lses, w_o)


WEIGHT_PART_LANES = N_HEADS


def _merge_prompt_kernel(x_ref, o0_ref, o1_ref, o2_ref, l0_ref, l1_ref, l2_ref, e_ref, w_ref, out_ref,
                         oscr1, oscr2, lscr1, lscr2, *, tm):
    o_refs, l_refs = (o0_ref, o1_ref, o2_ref), (l0_ref, l1_ref, l2_ref)
    o_scr, l_scr = (None, oscr1, oscr2), (None, lscr1, lscr2)
    os_, ls_ = [], []
    for g, dil in enumerate(DILATIONS):
        if dil == 1:
            os_.append(o_refs[g][0, 0].astype(F32))
            ls_.append(l_refs[g][0, 0])
        else:
            for r in range(dil):
                rows = pl.ds(r, tm // dil, stride=dil)
                o_r = o_refs[g][0, r].astype(F32)
                for c in range(ATTN_WIDTH // LANES):
                    o_scr[g][c, rows, :] = o_r[:, c * LANES:(c + 1) * LANES]
                l_scr[g][rows, :] = l_refs[g][0, r]
            os_.append(jnp.concatenate([o_scr[g][c] for c in range(ATTN_WIDTH // LANES)], axis=1))
            ls_.append(l_scr[g][...])
    m = jnp.maximum(jnp.maximum(ls_[0], ls_[1]), ls_[2])
    es = [jnp.exp(l - m) for l in ls_]
    den = es[0] + es[1] + es[2]
    lane = lax.broadcasted_iota(jnp.int32, (1, LANES), 1)
    head_lane = functools.reduce(jnp.logical_or, [lane == _lse_lane(h) for h in range(N_HEADS)])
    packed = None
    for g in range(N_BRANCH):
        w = jnp.where(head_lane, es[g] / den, 0.0)
        hi = w.astype(BF16).astype(F32)
        lo = (w - hi).astype(BF16).astype(F32)
        for part, val in ((2 * g, hi), (2 * g + 1, lo)):
            if part:
                val = pltpu.roll(val, part * WEIGHT_PART_LANES, axis=1)
            packed = val if packed is None else packed + val
    spread = jnp.dot(packed.astype(BF16), e_ref[...], preferred_element_type=F32)
    merged = sum(spread[:, g * ATTN_WIDTH:(g + 1) * ATTN_WIDTH] * os_[g] for g in range(N_BRANCH))
    out_ref[...] = x_ref[...] + jnp.dot(merged.astype(BF16), w_ref[...], preferred_element_type=F32)


def _merge_prompt(x, outs, lses, spread_mat, w_o, layer, *, bsz, seq, tm):
    n, d = x.shape
    nt = seq // tm
    tile = pl.BlockSpec((tm, d), lambda b, j: (b * nt + j, 0))
    cls = lambda width: [pl.BlockSpec((1, dil, tm // dil, width), lambda b, j: (b, 0, j, 0))
                         for dil in DILATIONS]
    return pl.pallas_call(
        functools.partial(_merge_prompt_kernel, tm=tm),
        grid=(bsz, nt),
        in_specs=[tile, *cls(ATTN_WIDTH), *cls(LANES),
                  pl.BlockSpec((LANES, N_BRANCH * ATTN_WIDTH), lambda b, j: (0, 0)),
                  pl.BlockSpec((None, d, d), lambda b, j: (layer, 0, 0))],
        out_specs=tile,
        out_shape=jax.ShapeDtypeStruct((n, d), F32),
        scratch_shapes=[pltpu.VMEM((ATTN_WIDTH // LANES, tm, LANES), F32),
                        pltpu.VMEM((ATTN_WIDTH // LANES, tm, LANES), F32),
                        pltpu.VMEM((tm, LANES), F32), pltpu.VMEM((tm, LANES), F32)],
        compiler_params=_cparams(2),
        name="merge_prompt",
    )(x, *outs, *lses, spread_mat, w_o)


def _rope_tables(pos):
    inv = jnp.power(ROPE_THETA, -jnp.arange(0, HEAD_DIM, 2, dtype=F32) / HEAD_DIM)
    ang = pos.astype(F32)[:, None] * inv[None, :]
    ang = jnp.concatenate([ang, ang], axis=-1)
    cos, sin = jnp.cos(ang), jnp.sin(ang)
    first = jnp.arange(HEAD_DIM) < HEAD_DIM // 2
    sa = jnp.where(first, -sin, 0.0)
    sb = jnp.where(first, 0.0, sin)
    tile2 = lambda a: jnp.concatenate([a, a], axis=-1)
    return tile2(cos), tile2(sa), tile2(sb)


def _head_mean_matrix():
    i = np.arange(MXU_N)
    return jnp.asarray((i[:, None] // HEAD_DIM == i[None, :] // HEAD_DIM) / HEAD_DIM, dtype=BF16)


def _prompt_bias():
    qi = np.arange(BLK)[:, None]
    ki = np.arange(2 * BLK)[None, :]
    rel = qi + BLK - ki
    band = (rel >= 0) & (rel <= BLK)
    first = band & (ki >= BLK)
    both = np.stack([np.tile(first, (2, 1)), np.tile(band, (2, 1))])
    return jnp.asarray(np.where(both, 0.0, NEG), dtype=F32)


def _sample_bias(hist, dil):
    t = np.arange(8)[:, None]
    l = np.arange(hist)[None, :]
    ok_c = (l >= t) & ((l - t) % dil == 0)
    return jnp.asarray(np.where(ok_c, 0.0, NEG), dtype=F32)


def _weight_spread_matrix():
    hit = np.zeros((LANES, N_BRANCH * ATTN_WIDTH), np.float32)
    used = set()
    for part in range(2 * N_BRANCH):
        for h in range(N_HEADS):
            lane = (_lse_lane(h) + part * WEIGHT_PART_LANES) % LANES
            assert lane not in used
            used.add(lane)
            col = (part // 2) * ATTN_WIDTH + h * HEAD_DIM
            hit[lane, col:col + HEAD_DIM] = 1.0
    return jnp.asarray(hit, dtype=BF16)


def kernel(x_prompt, x_sample, state_pool, cache_kv_w128, cache_kv_w512, cache_kv_w2048,
           a_norm, pool_w, pool_scale, kv_norm, w_kv, k_norm, b_norm, w_q, q_norm, w_o,
           ffn_norm, w_gate, w_up, w_down):
    bsz, seq, d = x_prompt.shape
    nreq, t_new, _ = x_sample.shape
    caches = (cache_kv_w128, cache_kv_w512, cache_kv_w2048)
    n_p, n_s = bsz * seq, nreq * t_new

    pool_w_b = pool_w.astype(BF16)
    w_kv_b, w_q_b, w_o_b = w_kv.astype(BF16), w_q.astype(BF16), w_o.astype(BF16)
    wg_b, wu_b, wd_b = w_gate.astype(BF16), w_up.astype(BF16), w_down.astype(BF16)

    tab_p = _rope_tables(jnp.arange(seq))
    tab_s = tuple(jnp.tile(a, (nreq, 1)) for a in _rope_tables(PAST_LEN + jnp.arange(t_new)))
    msum = _head_mean_matrix()
    k_gain = jnp.tile(k_norm[:, None, :], (1, N_HEADS, 1)).reshape(1, N_BRANCH * ATTN_WIDTH)
    bias_p = _prompt_bias()

    xp, xs = x_prompt, x_sample
    pool_p, pool_s = [], []
    for i in range(N_A_LAYERS):
        zeros_hist = jnp.zeros((bsz, HALO, d), F32)
        xp, hp = _pool_layer(xp, zeros_hist, a_norm[i:i + 1], pool_w_b[i], pool_scale[i:i + 1],
                             nb=1, tm=TM_PROMPT, full_hist=False)
        hist_s = jnp.pad(state_pool[:, i], ((0, 0), (1, 0), (0, 0)))
        xs, hs = _pool_layer(xs, hist_s, a_norm[i:i + 1], pool_w_b[i], pool_scale[i:i + 1],
                             nb=nreq, tm=t_new, full_hist=True)
        pool_p.append(hp[:, 1:])
        pool_s.append(hs[:, 1:])
        xp = _ffn(xp.reshape(n_p, d), ffn_norm, wg_b, wu_b, wd_b, i, tm=TM_PROMPT).reshape(bsz, seq, d)
        xs = _ffn(xs.reshape(n_s, d), ffn_norm, wg_b, wu_b, wd_b, i, tm=n_s).reshape(nreq, t_new, d)

    xp, xs = xp.reshape(n_p, d), xs.reshape(n_s, d)
    kv_norm2 = kv_norm.reshape(1, d)
    prompt = dict(bsz=bsz, seq=seq, tm=TM_PROMPT)
    *k_cls, kt0, kt1, kt2 = _proj_prompt(xp, kv_norm2, w_kv_b, 0, k_gain, tab_p, msum, rope=True,
                                         tail_kv=0, tail_arrays=None, **prompt)
    *v_cls, kvt0, kvt1, kvt2 = _proj_prompt(xp, kv_norm2, w_kv_b, 1, k_gain, tab_p, msum, rope=False,
                                            tail_kv=1, tail_arrays=(kt0, kt1, kt2), **prompt)
    kv_new_p = [t.reshape(bsz, 2, N_HEADS, HEAD_DIM, t.shape[-1]).transpose(0, 4, 1, 2, 3)
                for t in (kvt0, kvt1, kvt2)]
    k_cls = [a.reshape(bsz * dil, seq // dil, ATTN_WIDTH) for a, dil in zip(k_cls, DILATIONS)]
    v_cls = [a.reshape(bsz * dil, seq // dil, ATTN_WIDTH) for a, dil in zip(v_cls, DILATIONS)]

    kv_s = _proj_sample(xs, kv_norm2, w_kv_b, 0, k_gain, tab_s, msum, n_rope=PROJ_COLS)
    caches_t = [c.transpose(0, 2, 3, 4, 1) for c in caches]
    new_t = kv_s.T.reshape(2, N_BRANCH, N_HEADS, HEAD_DIM, n_s)
    sample_bias = [_sample_bias(c.shape[1], dil) for c, dil in zip(caches, DILATIONS)]
    spread_mat = _weight_spread_matrix()

    kv_sample_out = [None] * N_BRANCH
    for j in range(N_B_LAYERS):
        layer = N_A_LAYERS + j
        q_gain = jnp.tile(q_norm[j][:, None, :], (1, N_HEADS, 1)).reshape(1, PROJ_COLS)
        q_gain = q_gain * (HEAD_DIM ** -0.5)
        bn = b_norm[j:j + 1]
        q_cls = _proj_prompt(xp, bn, w_q_b, j, q_gain * LOG2E, tab_p, msum, rope=True,
                             tail_kv=None, tail_arrays=None, **prompt)
        q_s = _proj_sample(xs, bn, w_q_b, j, q_gain, tab_s, msum, n_rope=PROJ_COLS)
        q_s5 = q_s.reshape(nreq, t_new, N_BRANCH, N_HEADS, HEAD_DIM).transpose(2, 0, 3, 1, 4)

        outs_p, lses_p, outs_s, lses_s = [], [], [], []
        for g, dil in enumerate(DILATIONS):
            q_c = q_cls[g].reshape(bsz * dil, seq // dil, ATTN_WIDTH)
            o_c, lse_c = _attn(q_c, k_cls[g], v_cls[g], bias_p)
            outs_p.append(o_c.reshape(bsz, dil, seq // dil, ATTN_WIDTH))
            lses_p.append(lse_c.reshape(bsz, dil, seq // dil, LANES))

            res = _attn_sample(q_s5, caches_t[g], new_t, sample_bias[g], g, dil=dil, write_cache=(j == 0))
            outs_s.append(res[0].transpose(0, 2, 1, 3).reshape(n_s, ATTN_WIDTH))
            lses_s.append(res[1].transpose(0, 2, 1, 3).reshape(n_s, ATTN_WIDTH))
            if j == 0:
                kv_sample_out[g] = res[2].transpose(0, 4, 1, 2, 3)

        xp = _merge_prompt(xp, outs_p, lses_p, spread_mat, w_o_b, j, **prompt)
        xs = _merge_sample(xs, outs_s, lses_s, w_o_b, j)
        xp = _ffn(xp, ffn_norm, wg_b, wu_b, wd_b, layer, tm=TM_PROMPT)
        xs = _ffn(xs, ffn_norm, wg_b, wu_b, wd_b, layer, tm=n_s)

    return (xp.reshape(bsz, seq, d), xs.reshape(nreq, t_new, d),
            jnp.stack(pool_p, axis=1), jnp.stack(pool_s, axis=1),
            kv_new_p[0], kv_sample_out[0], kv_new_p[1], kv_sample_out[1],
            kv_new_p[2], kv_sample_out[2])
```

```python
import functools

import jax
import jax.numpy as jnp
import numpy as np
from jax import lax
from jax.experimental import pallas as pl
from jax.experimental.pallas import tpu as pltpu

F32 = jnp.float32
BF16 = jnp.bfloat16

D_MODEL = 1024
N_A_LAYERS = 2
N_B_LAYERS = 2
POOL_WINDOWS = (2, 4, 8, 16)
POOL_GROUP = 256
HALO = 16
POOL_HIST = 15
WINDOWS = (128, 512, 2048)
DILATIONS = (1, 4, 16)
N_BRANCH = 3
HEAD_DIM = 64
N_HEADS = 16
ATTN_WIDTH = N_HEADS * HEAD_DIM
D_FF = 2816
ROPE_THETA = 10000.0
EPS = 1e-6
PAST_LEN = 8192
BLK = 128
NEG = -1e30

LANES = 128
MXU_N = 256
VMEM_LIMIT = 56 * 1024 * 1024
TM_PROMPT = 512


def _cparams(n_axes):
    return pltpu.CompilerParams(dimension_semantics=("arbitrary",) * n_axes,
                                vmem_limit_bytes=VMEM_LIMIT)


def _rms(x, g):
    return x * lax.rsqrt(jnp.mean(x * x, axis=-1, keepdims=True) + EPS) * g


def _pool_kernel(x_ref, xh_ref, hist_ref, g_ref, w_ref, sc_ref, o_ref, ho_ref, *, tm, full_hist):
    j = pl.program_id(1)
    nb = x_ref.shape[0]
    x = x_ref[...]
    g = g_ref[...]
    u = _rms(x, g)
    hu = _rms(xh_ref[...], g)
    h = jnp.where(j == 0, hist_ref[...], hu)
    ext = jnp.concatenate([h, u], axis=1)
    ho_ref[...] = ext[:, tm:tm + HALO]
    s = ext
    sums = {}
    for step in (1, 2, 4, 8):
        s = s + pltpu.roll(s, step, axis=1)
        sums[2 * step] = s
    row = j * tm + lax.broadcasted_iota(jnp.int32, (1, tm, 1), 1)
    for gi, win in enumerate(POOL_WINDOWS):
        lo, hi = gi * POOL_GROUP, (gi + 1) * POOL_GROUP
        sw = sums[win][:, HALO:, lo:hi]
        if full_hist:
            cnt = jnp.float32(win)
        else:
            cnt = jnp.minimum(row + 1, win).astype(F32)
        d = (sw / cnt - u[:, :, lo:hi]).reshape(nb * tm, POOL_GROUP)
        y = jnp.dot(d.astype(BF16), w_ref[gi], preferred_element_type=F32).reshape(nb, tm, POOL_GROUP)
        o_ref[:, :, lo:hi] = x[:, :, lo:hi] + y * sc_ref[:, lo:hi]


def _pool_layer(x, hist, a_norm, w_pool, scale, *, nb, tm, full_hist):
    b, s, d = x.shape
    nt = s // tm
    if full_hist:
        halo_arr, halo_map = hist, (lambda bi, j: (bi, 0, 0))
    else:
        r = tm // HALO
        halo_arr, halo_map = x, (lambda bi, j: (bi, jnp.maximum(j * r - 1, 0), 0))
    return pl.pallas_call(
        functools.partial(_pool_kernel, tm=tm, full_hist=full_hist),
        grid=(b // nb, nt),
        in_specs=[
            pl.BlockSpec((nb, tm, d), lambda bi, j: (bi, j, 0)),
            pl.BlockSpec((nb, HALO, d), halo_map),
            pl.BlockSpec((nb, HALO, d), lambda bi, j: (bi, 0, 0)),
            pl.BlockSpec((1, d), lambda bi, j: (0, 0)),
            pl.BlockSpec((4, POOL_GROUP, POOL_GROUP), lambda bi, j: (0, 0, 0)),
            pl.BlockSpec((1, d), lambda bi, j: (0, 0)),
        ],
        out_specs=[
            pl.BlockSpec((nb, tm, d), lambda bi, j: (bi, j, 0)),
            pl.BlockSpec((nb, HALO, d), lambda bi, j: (bi, 0, 0)),
        ],
        out_shape=[jax.ShapeDtypeStruct((b, s, d), F32),
                   jax.ShapeDtypeStruct((b, HALO, d), F32)],
        compiler_params=_cparams(2),
        name="pool_layer",
    )(x, halo_arr, hist, a_norm, w_pool, scale)


FF_CHUNK = MXU_N


def _ffn_kernel(x_ref, g_ref, wg_ref, wu_ref, wd_ref, o_ref):
    x = x_ref[...]
    h = _rms(x, g_ref[...]).astype(BF16)
    acc = x
    for c in range(D_FF // FF_CHUNK):
        sl = slice(c * FF_CHUNK, (c + 1) * FF_CHUNK)
        a = jnp.dot(h, wg_ref[:, sl], preferred_element_type=F32)
        b = jnp.dot(h, wu_ref[:, sl], preferred_element_type=F32)
        t = (a * jax.nn.sigmoid(a) * b).astype(BF16)
        acc = acc + jnp.dot(t, wd_ref[sl, :], preferred_element_type=F32)
    o_ref[...] = acc


def _ffn(x, g, wg, wu, wd, layer, *, tm):
    n, d = x.shape
    g = g[layer:layer + 1]
    resident = dict(pipeline_mode=pl.Buffered(1))
    return pl.pallas_call(
        _ffn_kernel,
        grid=(n // tm,),
        in_specs=[
            pl.BlockSpec((tm, d), lambda i: (i, 0)),
            pl.BlockSpec((1, d), lambda i: (0, 0)),
            pl.BlockSpec((None, d, D_FF), lambda i: (layer, 0, 0), **resident),
            pl.BlockSpec((None, d, D_FF), lambda i: (layer, 0, 0), **resident),
            pl.BlockSpec((None, D_FF, d), lambda i: (layer, 0, 0), **resident),
        ],
        out_specs=pl.BlockSpec((tm, d), lambda i: (i, 0)),
        out_shape=jax.ShapeDtypeStruct((n, d), F32),
        compiler_params=_cparams(1),
        name="ffn",
    )(x, g, wg, wu, wd)


PROJ_COLS = N_BRANCH * ATTN_WIDTH


def _proj_dot(h, w_ref, c):
    return jnp.dot(h, w_ref[:, c * MXU_N:(c + 1) * MXU_N], preferred_element_type=F32)


def _head_norm(y, c, hg_ref, m_ref):
    msq = jnp.dot((y * y).astype(BF16), m_ref[...], preferred_element_type=F32)
    return y * lax.rsqrt(msq + EPS) * hg_ref[:, c * MXU_N:(c + 1) * MXU_N]


def _rope_natural(y, cos, sa, sb):
    halves = []
    for p in range(MXU_N // LANES):
        z = y[:, p * LANES:(p + 1) * LANES]
        halves.append(z * cos + pltpu.roll(z, LANES - HEAD_DIM // 2, axis=1) * sa
                      + pltpu.roll(z, HEAD_DIM // 2, axis=1) * sb)
    return jnp.concatenate(halves, axis=1)


def _rope_paired(y, cos, sn):
    halves = []
    for p in range(MXU_N // LANES):
        z = y[:, p * LANES:(p + 1) * LANES]
        halves.append(z * cos + pltpu.roll(z, LANES // 2, axis=1) * sn)
    return jnp.concatenate(halves, axis=1)


def _proj_sample_kernel(x_ref, g_ref, w_ref, hg_ref, cos_ref, sa_ref, sb_ref, m_ref, o_ref, *, n_rope):
    h = _rms(x_ref[...], g_ref[...]).astype(BF16)
    cos, sa, sb = cos_ref[...], sa_ref[...], sb_ref[...]
    for c in range(o_ref.shape[1] // MXU_N):
        y = _proj_dot(h, w_ref, c)
        if c * MXU_N < n_rope:
            y = _rope_natural(_head_norm(y, c, hg_ref, m_ref), cos, sa, sb)
        o_ref[:, c * MXU_N:(c + 1) * MXU_N] = y


def _proj_sample(x, g, w, w_index, hgain, tables, msum, *, n_rope):
    n, d = x.shape
    n_cols = w.shape[-1]
    full = lambda *shape: pl.BlockSpec(shape, lambda i: (0,) * len(shape))
    if w.ndim == 3:
        w_spec = pl.BlockSpec((None, d, n_cols), lambda i: (w_index, 0, 0))
    else:
        w_spec = full(d, n_cols)
    return pl.pallas_call(
        functools.partial(_proj_sample_kernel, n_rope=n_rope),
        grid=(1,),
        in_specs=[full(n, d), full(1, d), w_spec, full(1, n_rope),
                  full(n, LANES), full(n, LANES), full(n, LANES), full(MXU_N, MXU_N)],
        out_specs=full(n, n_cols),
        out_shape=jax.ShapeDtypeStruct((n, n_cols), F32),
        compiler_params=_cparams(1),
        name="proj_sample",
    )(x, g, w, hgain, *tables, msum)


def _proj_prompt_kernel(x_ref, g_ref, w_ref, hg_ref, cos_ref, sn_ref, m_ref, *refs,
                        tm, rope, tail_blocks, tail_start):
    outs, yscr = refs[:N_BRANCH], refs[-1]
    tails = refs[N_BRANCH:-1]
    j = pl.program_id(1)
    h = _rms(x_ref[...], g_ref[...]).astype(BF16)
    cos, sn = cos_ref[...], sn_ref[...]
    per_branch = ATTN_WIDTH // MXU_N
    slabs = MXU_N // LANES
    n_chunks = PROJ_COLS // MXU_N
    y_next = _proj_dot(h, w_ref, 0)
    for c in range(n_chunks):
        g, cc = divmod(c, per_branch)
        y = y_next
        if c + 1 < n_chunks:
            y_next = _proj_dot(h, w_ref, c + 1)
        if rope:
            y = _rope_paired(_head_norm(y, c, hg_ref, m_ref), cos, sn)
        dil = DILATIONS[g]
        if dil == 1:
            outs[g][0, 0, :, cc * MXU_N:(cc + 1) * MXU_N] = y.astype(BF16)
        if dil > 1 or tails:
            for p in range(slabs):
                yscr[c * slabs + p] = y[:, p * LANES:(p + 1) * LANES]
        if dil > 1:
            for r in range(dil):
                for p in range(slabs):
                    lo = cc * MXU_N + p * LANES
                    outs[g][0, r, :, lo:lo + LANES] = (
                        yscr[c * slabs + p, pl.ds(r, tm // dil, stride=dil), :].astype(BF16))
    for g in range(N_BRANCH if tails else 0):
        blk = tail_blocks[g]

        @pl.when(j >= tail_start[g])
        def _(g=g, blk=blk):
            quarter = LANES // 4
            for s in range(ATTN_WIDTH // LANES):
                t = yscr[g * (ATTN_WIDTH // LANES) + s, tm - blk:, :].T
                if not rope:
                    tails[g][0, s * LANES:(s + 1) * LANES, :] = t
                    continue
                for half in range(2):
                    for hh in range(2):
                        src = (2 * half + hh) * quarter
                        dst = s * LANES + (2 * hh + half) * quarter
                        tails[g][0, dst:dst + quarter, :] = t[src:src + quarter, :]


def _proj_prompt(x, g, w, w_block, hgain, tables, msum, *, bsz, seq, tm, rope, want_tails):
    n, d = x.shape
    nt = seq // tm
    if w.ndim == 3:
        w_spec = pl.BlockSpec((None, d, PROJ_COLS), lambda b, j: (w_block, 0, 0), pipeline_mode=pl.Buffered(1))
    else:
        w_spec = pl.BlockSpec((d, PROJ_COLS), lambda b, j: (0, w_block), pipeline_mode=pl.Buffered(1))
    const = lambda *shape: pl.BlockSpec(shape, lambda b, j: (0,) * len(shape))
    tab = pl.BlockSpec((tm, LANES), lambda b, j: (j, 0))
    in_specs = [pl.BlockSpec((tm, d), lambda b, j: (b * nt + j, 0)), const(1, d), w_spec,
                const(1, PROJ_COLS), tab, tab, const(MXU_N, MXU_N)]
    out_specs = [pl.BlockSpec((1, dil, tm // dil, ATTN_WIDTH), lambda b, j: (b, 0, j, 0)) for dil in DILATIONS]
    out_shape = [jax.ShapeDtypeStruct((bsz, dil, seq // dil, ATTN_WIDTH), BF16) for dil in DILATIONS]
    tail_blocks = tail_start = None
    if want_tails:
        keeps = [min(wd, seq) for wd in WINDOWS]
        tail_blocks = tuple(min(k, tm) for k in keeps)
        tail_start = tuple(nt - max(k // tm, 1) for k in keeps)
        for k, blk, j0 in zip(keeps, tail_blocks, tail_start):
            out_specs.append(pl.BlockSpec((1, ATTN_WIDTH, blk),
                                          lambda b, j, j0=j0: (b, 0, jnp.maximum(j - j0, 0))))
            out_shape.append(jax.ShapeDtypeStruct((bsz, ATTN_WIDTH, k), F32))
    return pl.pallas_call(
        functools.partial(_proj_prompt_kernel, tm=tm, rope=rope, tail_blocks=tail_blocks, tail_start=tail_start),
        grid=(bsz, nt),
        in_specs=in_specs,
        out_specs=out_specs,
        out_shape=out_shape,
        scratch_shapes=[pltpu.VMEM((PROJ_COLS // LANES, tm, LANES), F32)],
        compiler_params=_cparams(2),
        name="proj_prompt",
    )(x, g, w, hgain, *tables, msum)


ATTN_Q_BLOCKS = 2
LN2 = float(np.log(2.0))
LOG2E = float(1.0 / np.log(2.0))


def _lse_lane(head):
    return head if head % 2 == 0 else HEAD_DIM + head


def _attn_kernel(q_ref, kp_ref, kc_ref, vp_ref, vc_ref, bias_ref, o_ref, lse_ref):
    j = pl.program_id(1)
    lane = lax.broadcasted_iota(jnp.int32, (BLK, LANES), 1)
    first = lane < HEAD_DIM
    first_qk = (lane & (HEAD_DIM // 2)) == 0
    nt = (((1,), (1,)), ((), ()))
    bias_first = bias_ref[jnp.minimum(j, 1)]

    def keys_values(b, p, ref_prev, ref_cur):
        sl = slice(p * LANES, (p + 1) * LANES)
        prev = ref_prev[0, :, sl] if b == 0 else ref_cur[0, (b - 1) * BLK:b * BLK, sl]
        return jnp.concatenate([prev, ref_cur[0, b * BLK:(b + 1) * BLK, sl]], axis=0)

    def scores(b, p):
        q2 = q_ref[0, b * BLK:(b + 1) * BLK, p * LANES:(p + 1) * LANES]
        zero = jnp.zeros_like(q2)
        qq = jnp.concatenate([jnp.where(first_qk, q2, zero), jnp.where(first_qk, zero, q2)], axis=0)
        kk = keys_values(b, p, kp_ref, kc_ref)
        return lax.dot_general(qq, kk, nt, preferred_element_type=F32) + (bias_first if b == 0 else bias_ref[1])

    pairs = [(b, p) for b in range(ATTN_Q_BLOCKS) for p in range(N_HEADS // 2)]
    s_next = scores(*pairs[0])
    lse_all = None
    for i, (b, p) in enumerate(pairs):
        s = s_next
        if i + 1 < len(pairs):
            s_next = scores(*pairs[i + 1])
        m = jnp.max(s, axis=-1, keepdims=True)
        e = jnp.exp2(s - m).astype(BF16)
        vv = keys_values(b, p, vp_ref, vc_ref)
        v1 = jnp.concatenate([vv, jnp.ones_like(vv)], axis=1)
        pv = jnp.dot(e, v1, preferred_element_type=F32)
        rows, sl = slice(b * BLK, (b + 1) * BLK), slice(p * LANES, (p + 1) * LANES)
        num = jnp.where(first, pv[:BLK, :LANES], pv[BLK:, :LANES])
        den = jnp.where(first, pv[:BLK, LANES:], pv[BLK:, LANES:])
        o_ref[0, rows, sl] = (num / den).astype(o_ref.dtype)
        lse_pair = (jnp.where(first, m[:BLK], m[BLK:]) + jnp.log2(den)) * LN2
        here = (lane == _lse_lane(2 * p)) | (lane == _lse_lane(2 * p + 1))
        lse_all = jnp.where(here, lse_pair, jnp.zeros_like(lse_pair) if p == 0 else lse_all)
        if p == N_HEADS // 2 - 1:
            lse_ref[0, rows, :] = lse_all


def _attn(q, k, v, bias):
    ns, l, w = q.shape
    nq = ATTN_Q_BLOCKS
    cur = lambda s, j: (s, j, 0)
    prev = lambda s, j: (s, jnp.maximum(j * nq - 1, 0), 0)
    big, small = (1, nq * BLK, w), (1, BLK, w)
    return pl.pallas_call(
        _attn_kernel,
        grid=(ns, l // (nq * BLK)),
        in_specs=[
            pl.BlockSpec(big, cur),
            pl.BlockSpec(small, prev), pl.BlockSpec(big, cur),
            pl.BlockSpec(small, prev), pl.BlockSpec(big, cur),
            pl.BlockSpec((2, 2 * BLK, 2 * BLK), lambda s, j: (0, 0, 0)),
        ],
        out_specs=[pl.BlockSpec(big, cur), pl.BlockSpec((1, nq * BLK, LANES), cur)],
        out_shape=[jax.ShapeDtypeStruct((ns, l, w), BF16), jax.ShapeDtypeStruct((ns, l, LANES), F32)],
        compiler_params=_cparams(2),
        name="attn_prompt",
    )(q, k, k, v, v, bias)


SAMPLE_CACHE_BLOCK_POSITIONS = 8192


def _attn_sample_kernel(q_ref, cache_ref, new_ref, bias_c_ref, o_ref, lse_ref, *cache_out, hist, dil, hb):
    n = pl.program_id(0)
    slot = (n % (LANES // 8)) * 8
    lane = lax.broadcasted_iota(jnp.int32, (8, LANES), 1)
    t_q = lax.broadcasted_iota(jnp.int32, (8, LANES), 0)
    diff = t_q - (lane - slot)
    ok = (diff >= 0) & (diff <= t_q) & ((diff & (dil - 1)) == 0)
    bias_n = jnp.where(ok, 0.0, NEG)
    bias_c = bias_c_ref[...]
    lane_c = lax.broadcasted_iota(jnp.int32, (HEAD_DIM, LANES), 1)
    nt = (((1,), (1,)), ((), ()))
    s_cs = [jnp.dot(q_ref[0, h], cache_ref[0, 0, h], preferred_element_type=F32) + bias_c for h in range(hb)]
    s_ns = [jnp.dot(q_ref[0, h], new_ref[0, h], preferred_element_type=F32) + bias_n for h in range(hb)]
    ms = [jnp.maximum(jnp.max(s_c, axis=-1, keepdims=True), jnp.max(s_n, axis=-1, keepdims=True))
          for s_c, s_n in zip(s_cs, s_ns)]
    e_cs = [jnp.exp(s_c - m) for s_c, m in zip(s_cs, ms)]
    e_ns = [jnp.exp(s_n - m) for s_n, m in zip(s_ns, ms)]
    ls = [jnp.sum(e_c, axis=-1, keepdims=True) + jnp.sum(e_n, axis=-1, keepdims=True)
          for e_c, e_n in zip(e_cs, e_ns)]
    for h in range(hb):
        o = (lax.dot_general(e_cs[h], cache_ref[0, 1, h], nt, preferred_element_type=F32)
             + lax.dot_general(e_ns[h], new_ref[1, h], nt, preferred_element_type=F32))
        o_ref[0, h] = o / ls[h]
        lse_ref[0, h] = jnp.broadcast_to(ms[h] + jnp.log(ls[h]), (8, HEAD_DIM))
    for out_ref in cache_out:
        for h in range(hb):
            for kv in range(2):
                shifted = pltpu.roll(cache_ref[0, kv, h], hist - 8, axis=1)
                if hist > LANES:
                    out_ref[0, kv, h, :, :hist - LANES] = shifted[:, :hist - LANES]
                moved = pltpu.roll(new_ref[kv, h], (LANES - 8) - slot, axis=1)
                out_ref[0, kv, h, :, hist - LANES:] = jnp.where(
                    lane_c >= LANES - 8, moved, shifted[:, hist - LANES:])


def _attn_sample(q, cache, new_t, bias_c, branch, *, dil, write_cache):
    nreq = q.shape[1]
    hist = cache.shape[-1]
    hb = min(N_HEADS, SAMPLE_CACHE_BLOCK_POSITIONS // hist)
    o_spec = pl.BlockSpec((1, hb, 8, HEAD_DIM), lambda n, h: (n, h, 0, 0))
    out_specs = [o_spec, o_spec]
    out_shape = [jax.ShapeDtypeStruct((nreq, N_HEADS, 8, HEAD_DIM), F32)] * 2
    if write_cache:
        out_specs.append(pl.BlockSpec((1, 2, hb, HEAD_DIM, hist), lambda n, h: (n, 0, h, 0, 0)))
        out_shape.append(jax.ShapeDtypeStruct(cache.shape, F32))
    return pl.pallas_call(
        functools.partial(_attn_sample_kernel, hist=hist, dil=dil, hb=hb),
        grid=(nreq, N_HEADS // hb),
        in_specs=[
            pl.BlockSpec((None, 1, hb, 8, HEAD_DIM), lambda n, h: (branch, n, h, 0, 0)),
            pl.BlockSpec((1, 2, hb, HEAD_DIM, hist), lambda n, h: (n, 0, h, 0, 0)),
            pl.BlockSpec((2, None, hb, HEAD_DIM, LANES), lambda n, h: (0, branch, h, 0, n // (LANES // 8))),
            pl.BlockSpec((8, hist), lambda n, h: (0, 0)),
        ],
        out_specs=out_specs,
        out_shape=out_shape,
        compiler_params=_cparams(2),
        name="attn_sample",
    )(q, cache, new_t, bias_c)


def _merge_sample_kernel(x_ref, o0, o1, o2, l0, l1, l2, w_ref, out_ref):
    ls = [l0[...], l1[...], l2[...]]
    m = jnp.maximum(jnp.maximum(ls[0], ls[1]), ls[2])
    es = [jnp.exp(l - m) for l in ls]
    den = es[0] + es[1] + es[2]
    num = es[0] * o0[...] + es[1] * o1[...] + es[2] * o2[...]
    merged = (num / den).astype(BF16)
    out_ref[...] = x_ref[...] + jnp.dot(merged, w_ref[...], preferred_element_type=F32)


def _merge_sample(x, outs, lses, w_o, layer):
    n, d = x.shape
    tile = pl.BlockSpec((n, d), lambda i: (0, 0))
    return pl.pallas_call(
        _merge_sample_kernel,
        grid=(1,),
        in_specs=[tile] * 7 + [pl.BlockSpec((None, d, d), lambda i: (layer, 0, 0))],
        out_specs=tile,
        out_shape=jax.ShapeDtypeStruct((n, d), F32),
        compiler_params=_cparams(1),
        name="merge_sample",
    )(x, *outs, *lses, w_o)


WEIGHT_PART_LANES = N_HEADS


def _merge_prompt_kernel(x_ref, o0_ref, o1_ref, o2_ref, l0_ref, l1_ref, l2_ref, e_ref, w_ref, out_ref,
                         oscr1, oscr2, lscr1, lscr2, *, tm):
    o_refs, l_refs = (o0_ref, o1_ref, o2_ref), (l0_ref, l1_ref, l2_ref)
    o_scr, l_scr = (None, oscr1, oscr2), (None, lscr1, lscr2)
    os_, ls_ = [], []
    for g, dil in enumerate(DILATIONS):
        if dil == 1:
            os_.append(o_refs[g][0, 0].astype(F32))
            ls_.append(l_refs[g][0, 0])
        else:
            for r in range(dil):
                rows = pl.ds(r, tm // dil, stride=dil)
                o_r = o_refs[g][0, r].astype(F32)
                for c in range(ATTN_WIDTH // LANES):
                    o_scr[g][c, rows, :] = o_r[:, c * LANES:(c + 1) * LANES]
                l_scr[g][rows, :] = l_refs[g][0, r]
            os_.append(jnp.concatenate([o_scr[g][c] for c in range(ATTN_WIDTH // LANES)], axis=1))
            ls_.append(l_scr[g][...])
    m = jnp.maximum(jnp.maximum(ls_[0], ls_[1]), ls_[2])
    es = [jnp.exp(l - m) for l in ls_]
    den = es[0] + es[1] + es[2]
    lane = lax.broadcasted_iota(jnp.int32, (1, LANES), 1)
    head_lane = functools.reduce(jnp.logical_or, [lane == _lse_lane(h) for h in range(N_HEADS)])
    packed = None
    for g in range(N_BRANCH):
        w = jnp.where(head_lane, es[g] / den, 0.0)
        hi = w.astype(BF16).astype(F32)
        lo = (w - hi).astype(BF16).astype(F32)
        for part, val in ((2 * g, hi), (2 * g + 1, lo)):
            if part:
                val = pltpu.roll(val, part * WEIGHT_PART_LANES, axis=1)
            packed = val if packed is None else packed + val
    spread = jnp.dot(packed.astype(BF16), e_ref[...], preferred_element_type=F32)
    merged = sum(spread[:, g * ATTN_WIDTH:(g + 1) * ATTN_WIDTH] * os_[g] for g in range(N_BRANCH))
    out_ref[...] = x_ref[...] + jnp.dot(merged.astype(BF16), w_ref[...], preferred_element_type=F32)


def _merge_prompt(x, outs, lses, spread_mat, w_o, layer, *, bsz, seq, tm):
    n, d = x.shape
    nt = seq // tm
    tile = pl.BlockSpec((tm, d), lambda b, j: (b * nt + j, 0))
    cls = lambda width: [pl.BlockSpec((1, dil, tm // dil, width), lambda b, j: (b, 0, j, 0))
                         for dil in DILATIONS]
    return pl.pallas_call(
        functools.partial(_merge_prompt_kernel, tm=tm),
        grid=(bsz, nt),
        in_specs=[tile, *cls(ATTN_WIDTH), *cls(LANES),
                  pl.BlockSpec((LANES, N_BRANCH * ATTN_WIDTH), lambda b, j: (0, 0)),
                  pl.BlockSpec((None, d, d), lambda b, j: (layer, 0, 0))],
        out_specs=tile,
        out_shape=jax.ShapeDtypeStruct((n, d), F32),
        scratch_shapes=[pltpu.VMEM((ATTN_WIDTH // LANES, tm, LANES), F32),
                        pltpu.VMEM((ATTN_WIDTH // LANES, tm, LANES), F32),
                        pltpu.VMEM((tm, LANES), F32), pltpu.VMEM((tm, LANES), F32)],
        compiler_params=_cparams(2),
        name="merge_prompt",
    )(x, *outs, *lses, spread_mat, w_o)


def _rope_tables(pos):
    inv = jnp.power(ROPE_THETA, -jnp.arange(0, HEAD_DIM, 2, dtype=F32) / HEAD_DIM)
    ang = pos.astype(F32)[:, None] * inv[None, :]
    ang = jnp.concatenate([ang, ang], axis=-1)
    cos, sin = jnp.cos(ang), jnp.sin(ang)
    first = jnp.arange(HEAD_DIM) < HEAD_DIM // 2
    sa = jnp.where(first, -sin, 0.0)
    sb = jnp.where(first, 0.0, sin)
    tile2 = lambda a: jnp.concatenate([a, a], axis=-1)
    return tile2(cos), tile2(sa), tile2(sb)


def _rope_tables_paired(pos):
    inv = jnp.power(ROPE_THETA, -jnp.arange(0, HEAD_DIM, 2, dtype=F32) / HEAD_DIM)
    ang = jnp.tile(pos.astype(F32)[:, None] * inv[None, :], (1, 4))
    sign = jnp.where(jnp.arange(LANES) < LANES // 2, -1.0, 1.0)
    return jnp.cos(ang), jnp.sin(ang) * sign


def _paired_columns():
    half = HEAD_DIM // 2
    nat = np.arange(PROJ_COLS).reshape(PROJ_COLS // LANES, 2, 2, half)
    return nat.transpose(0, 2, 1, 3).reshape(-1)


def _head_mean_matrix(paired):
    i = np.arange(MXU_N)
    head = (i // LANES) * 2 + (i // (HEAD_DIM // 2)) % 2 if paired else i // HEAD_DIM
    return jnp.asarray((head[:, None] == head[None, :]) / HEAD_DIM, dtype=BF16)


def _prompt_bias():
    qi = np.arange(BLK)[:, None]
    ki = np.arange(2 * BLK)[None, :]
    rel = qi + BLK - ki
    band = (rel >= 0) & (rel <= BLK)
    first = band & (ki >= BLK)
    both = np.stack([np.tile(first, (2, 1)), np.tile(band, (2, 1))])
    return jnp.asarray(np.where(both, 0.0, NEG), dtype=F32)


def _sample_bias(hist, dil):
    t = np.arange(8)[:, None]
    l = np.arange(hist)[None, :]
    ok_c = (l >= t) & ((l - t) % dil == 0)
    return jnp.asarray(np.where(ok_c, 0.0, NEG), dtype=F32)


def _weight_spread_matrix():
    hit = np.zeros((LANES, N_BRANCH * ATTN_WIDTH), np.float32)
    used = set()
    for part in range(2 * N_BRANCH):
        for h in range(N_HEADS):
            lane = (_lse_lane(h) + part * WEIGHT_PART_LANES) % LANES
            assert lane not in used
            used.add(lane)
            col = (part // 2) * ATTN_WIDTH + h * HEAD_DIM
            hit[lane, col:col + HEAD_DIM] = 1.0
    return jnp.asarray(hit, dtype=BF16)


def kernel(x_prompt, x_sample, state_pool, cache_kv_w128, cache_kv_w512, cache_kv_w2048,
           a_norm, pool_w, pool_scale, kv_norm, w_kv, k_norm, b_norm, w_q, q_norm, w_o,
           ffn_norm, w_gate, w_up, w_down):
    bsz, seq, d = x_prompt.shape
    nreq, t_new, _ = x_sample.shape
    caches = (cache_kv_w128, cache_kv_w512, cache_kv_w2048)
    n_p, n_s = bsz * seq, nreq * t_new

    pool_w_b = pool_w.astype(BF16)
    w_kv_b, w_q_b, w_o_b = w_kv.astype(BF16), w_q.astype(BF16), w_o.astype(BF16)
    wg_b, wu_b, wd_b = w_gate.astype(BF16), w_up.astype(BF16), w_down.astype(BF16)

    paired = _paired_columns()
    w_k_pb = w_kv[:, :PROJ_COLS][:, paired].astype(BF16)
    w_q_pb = w_q[:, :, paired].astype(BF16)
    tab_p = _rope_tables_paired(jnp.arange(seq))
    tab_s = tuple(jnp.tile(a, (nreq, 1)) for a in _rope_tables(PAST_LEN + jnp.arange(t_new)))
    msum, msum_p = _head_mean_matrix(False), _head_mean_matrix(True)
    k_gain = jnp.tile(k_norm[:, None, :], (1, N_HEADS, 1)).reshape(1, PROJ_COLS)
    bias_p = _prompt_bias()

    xp, xs = x_prompt, x_sample
    pool_p, pool_s = [], []
    for i in range(N_A_LAYERS):
        zeros_hist = jnp.zeros((bsz, HALO, d), F32)
        xp, hp = _pool_layer(xp, zeros_hist, a_norm[i:i + 1], pool_w_b[i], pool_scale[i:i + 1],
                             nb=1, tm=TM_PROMPT, full_hist=False)
        hist_s = jnp.pad(state_pool[:, i], ((0, 0), (1, 0), (0, 0)))
        xs, hs = _pool_layer(xs, hist_s, a_norm[i:i + 1], pool_w_b[i], pool_scale[i:i + 1],
                             nb=nreq, tm=t_new, full_hist=True)
        pool_p.append(hp[:, 1:])
        pool_s.append(hs[:, 1:])
        xp = _ffn(xp.reshape(n_p, d), ffn_norm, wg_b, wu_b, wd_b, i, tm=TM_PROMPT).reshape(bsz, seq, d)
        xs = _ffn(xs.reshape(n_s, d), ffn_norm, wg_b, wu_b, wd_b, i, tm=n_s).reshape(nreq, t_new, d)

    xp, xs = xp.reshape(n_p, d), xs.reshape(n_s, d)
    kv_norm2 = kv_norm.reshape(1, d)
    prompt = dict(bsz=bsz, seq=seq, tm=TM_PROMPT)
    k_out = _proj_prompt(xp, kv_norm2, w_k_pb, 0, k_gain[:, paired], tab_p, msum_p, rope=True,
                         want_tails=True, **prompt)
    v_out = _proj_prompt(xp, kv_norm2, w_kv_b, 1, k_gain, tab_p, msum_p, rope=False,
                         want_tails=True, **prompt)
    k_cls, v_cls = k_out[:N_BRANCH], v_out[:N_BRANCH]
    kv_new_p = [jnp.stack([kt, vt], axis=1).reshape(bsz, 2, N_HEADS, HEAD_DIM, kt.shape[-1]).transpose(0, 4, 1, 2, 3)
                for kt, vt in zip(k_out[N_BRANCH:], v_out[N_BRANCH:])]
    k_cls = [a.reshape(bsz * dil, seq // dil, ATTN_WIDTH) for a, dil in zip(k_cls, DILATIONS)]
    v_cls = [a.reshape(bsz * dil, seq // dil, ATTN_WIDTH) for a, dil in zip(v_cls, DILATIONS)]

    kv_s = _proj_sample(xs, kv_norm2, w_kv_b, 0, k_gain, tab_s, msum, n_rope=PROJ_COLS)
    caches_t = [c.transpose(0, 2, 3, 4, 1) for c in caches]
    new_t = kv_s.T.reshape(2, N_BRANCH, N_HEADS, HEAD_DIM, n_s)
    sample_bias = [_sample_bias(c.shape[1], dil) for c, dil in zip(caches, DILATIONS)]
    spread_mat = _weight_spread_matrix()

    kv_sample_out = [None] * N_BRANCH
    for j in range(N_B_LAYERS):
        layer = N_A_LAYERS + j
        q_gain = jnp.tile(q_norm[j][:, None, :], (1, N_HEADS, 1)).reshape(1, PROJ_COLS)
        q_gain = q_gain * (HEAD_DIM ** -0.5)
        bn = b_norm[j:j + 1]
        q_cls = _proj_prompt(xp, bn, w_q_pb, j, (q_gain * LOG2E)[:, paired], tab_p, msum_p, rope=True,
                             want_tails=False, **prompt)
        q_s = _proj_sample(xs, bn, w_q_b, j, q_gain, tab_s, msum, n_rope=PROJ_COLS)
        q_s5 = q_s.reshape(nreq, t_new, N_BRANCH, N_HEADS, HEAD_DIM).transpose(2, 0, 3, 1, 4)

        outs_p, lses_p, outs_s, lses_s = [], [], [], []
        for g, dil in enumerate(DILATIONS):
            q_c = q_cls[g].reshape(bsz * dil, seq // dil, ATTN_WIDTH)
            o_c, lse_c = _attn(q_c, k_cls[g], v_cls[g], bias_p)
            outs_p.append(o_c.reshape(bsz, dil, seq // dil, ATTN_WIDTH))
            lses_p.append(lse_c.reshape(bsz, dil, seq // dil, LANES))

            res = _attn_sample(q_s5, caches_t[g], new_t, sample_bias[g], g, dil=dil, write_cache=(j == 0))
            outs_s.append(res[0].transpose(0, 2, 1, 3).reshape(n_s, ATTN_WIDTH))
            lses_s.append(res[1].transpose(0, 2, 1, 3).reshape(n_s, ATTN_WIDTH))
            if j == 0:
                kv_sample_out[g] = res[2].transpose(0, 4, 1, 2, 3)

        xp = _merge_prompt(xp, outs_p, lses_p, spread_mat, w_o_b, j, **prompt)
        xs = _merge_sample(xs, outs_s, lses_s, w_o_b, j)
        xp = _ffn(xp, ffn_norm, wg_b, wu_b, wd_b, layer, tm=TM_PROMPT)
        xs = _ffn(xs, ffn_norm, wg_b, wu_b, wd_b, layer, tm=n_s)

    return (xp.reshape(bsz, seq, d), xs.reshape(nreq, t_new, d),
            jnp.stack(pool_p, axis=1), jnp.stack(pool_s, axis=1),
            kv_new_p[0], kv_sample_out[0], kv_new_p[1], kv_sample_out[1],
            kv_new_p[2], kv_sample_out[2])
```

```python
import functools

import jax
import jax.numpy as jnp
import numpy as np
from jax import lax
from jax.experimental import pallas as pl
from jax.experimental.pallas import tpu as pltpu

F32 = jnp.float32
BF16 = jnp.bfloat16

D_MODEL = 1024
N_A_LAYERS = 2
N_B_LAYERS = 2
POOL_WINDOWS = (2, 4, 8, 16)
POOL_GROUP = 256
HALO = 16
POOL_HIST = 15
WINDOWS = (128, 512, 2048)
DILATIONS = (1, 4, 16)
N_BRANCH = 3
HEAD_DIM = 64
N_HEADS = 16
ATTN_WIDTH = N_HEADS * HEAD_DIM
D_FF = 2816
ROPE_THETA = 10000.0
EPS = 1e-6
PAST_LEN = 8192
BLK = 128
NEG = -1e30

LANES = 128
MXU_N = 256
VMEM_LIMIT = 56 * 1024 * 1024
TM_PROMPT = 512


def _cparams(n_axes):
    return pltpu.CompilerParams(dimension_semantics=("arbitrary",) * n_axes,
                                vmem_limit_bytes=VMEM_LIMIT)


def _rms(x, g):
    return x * lax.rsqrt(jnp.mean(x * x, axis=-1, keepdims=True) + EPS) * g


FF_CHUNK = MXU_N


def _swiglu_residual(x, g_ref, wg_ref, wu_ref, wd_ref):
    h = _rms(x, g_ref[...]).astype(BF16)
    acc = x
    for c in range(D_FF // FF_CHUNK):
        sl = slice(c * FF_CHUNK, (c + 1) * FF_CHUNK)
        a = jnp.dot(h, wg_ref[:, sl], preferred_element_type=F32)
        b = jnp.dot(h, wu_ref[:, sl], preferred_element_type=F32)
        t = (a * jax.nn.sigmoid(a) * b).astype(BF16)
        acc = acc + jnp.dot(t, wd_ref[sl, :], preferred_element_type=F32)
    return acc


def _ffn_weight_specs(layer):
    idx = lambda *_: (layer, 0, 0)
    resident = dict(pipeline_mode=pl.Buffered(1))
    return [pl.BlockSpec((1, D_MODEL), lambda *_: (0, 0)),
            pl.BlockSpec((None, D_MODEL, D_FF), idx, **resident),
            pl.BlockSpec((None, D_MODEL, D_FF), idx, **resident),
            pl.BlockSpec((None, D_FF, D_MODEL), idx, **resident)]


def _pool_ffn_kernel(x_ref, xh_ref, hist_ref, g_ref, w_ref, sc_ref, fg_ref, wg_ref, wu_ref, wd_ref,
                     o_ref, ho_ref, *, tm, full_hist):
    j = pl.program_id(1)
    nb = x_ref.shape[0]
    x = x_ref[...]
    g = g_ref[...]
    u = _rms(x, g)
    hu = _rms(xh_ref[...], g)
    h = jnp.where(j == 0, hist_ref[...], hu)
    ext = jnp.concatenate([h, u], axis=1)
    ho_ref[...] = ext[:, tm:tm + HALO]
    s = ext
    sums = {}
    for step in (1, 2, 4, 8):
        s = s + pltpu.roll(s, step, axis=1)
        sums[2 * step] = s
    row = j * tm + lax.broadcasted_iota(jnp.int32, (1, tm, 1), 1)
    mixed = []
    for gi, win in enumerate(POOL_WINDOWS):
        lo, hi = gi * POOL_GROUP, (gi + 1) * POOL_GROUP
        sw = sums[win][:, HALO:, lo:hi]
        if full_hist:
            cnt = jnp.float32(win)
        else:
            cnt = jnp.minimum(row + 1, win).astype(F32)
        d = (sw / cnt - u[:, :, lo:hi]).reshape(nb * tm, POOL_GROUP)
        y = jnp.dot(d.astype(BF16), w_ref[gi], preferred_element_type=F32)
        mixed.append(x[:, :, lo:hi].reshape(nb * tm, POOL_GROUP) + y * sc_ref[:, lo:hi])
    x1 = jnp.concatenate(mixed, axis=1)
    o_ref[...] = _swiglu_residual(x1, fg_ref, wg_ref, wu_ref, wd_ref).reshape(nb, tm, D_MODEL)


def _pool_ffn_layer(x, hist, a_norm, w_pool, scale, ffn_gain, wg, wu, wd, layer, *, nb, tm, full_hist):
    b, s, d = x.shape
    nt = s // tm
    if full_hist:
        halo_arr, halo_map = hist, (lambda bi, j: (bi, 0, 0))
    else:
        r = tm // HALO
        halo_arr, halo_map = x, (lambda bi, j: (bi, jnp.maximum(j * r - 1, 0), 0))
    return pl.pallas_call(
        functools.partial(_pool_ffn_kernel, tm=tm, full_hist=full_hist),
        grid=(b // nb, nt),
        in_specs=[
            pl.BlockSpec((nb, tm, d), lambda bi, j: (bi, j, 0)),
            pl.BlockSpec((nb, HALO, d), halo_map),
            pl.BlockSpec((nb, HALO, d), lambda bi, j: (bi, 0, 0)),
            pl.BlockSpec((1, d), lambda bi, j: (0, 0)),
            pl.BlockSpec((4, POOL_GROUP, POOL_GROUP), lambda bi, j: (0, 0, 0)),
            pl.BlockSpec((1, d), lambda bi, j: (0, 0)),
            *_ffn_weight_specs(layer),
        ],
        out_specs=[
            pl.BlockSpec((nb, tm, d), lambda bi, j: (bi, j, 0)),
            pl.BlockSpec((nb, HALO, d), lambda bi, j: (bi, 0, 0)),
        ],
        out_shape=[jax.ShapeDtypeStruct((b, s, d), F32),
                   jax.ShapeDtypeStruct((b, HALO, d), F32)],
        compiler_params=_cparams(2),
        name="pool_ffn_layer",
    )(x, halo_arr, hist, a_norm, w_pool, scale, ffn_gain[layer:layer + 1], wg, wu, wd)


def _ffn_kernel(x_ref, g_ref, wg_ref, wu_ref, wd_ref, o_ref):
    o_ref[...] = _swiglu_residual(x_ref[...], g_ref, wg_ref, wu_ref, wd_ref)


def _ffn(x, g, wg, wu, wd, layer, *, tm):
    n, d = x.shape
    return pl.pallas_call(
        _ffn_kernel,
        grid=(n // tm,),
        in_specs=[pl.BlockSpec((tm, d), lambda i: (i, 0)), *_ffn_weight_specs(layer)],
        out_specs=pl.BlockSpec((tm, d), lambda i: (i, 0)),
        out_shape=jax.ShapeDtypeStruct((n, d), F32),
        compiler_params=_cparams(1),
        name="ffn",
    )(x, g[layer:layer + 1], wg, wu, wd)


PROJ_COLS = N_BRANCH * ATTN_WIDTH


def _proj_dot(h, w_ref, c):
    return jnp.dot(h, w_ref[:, c * MXU_N:(c + 1) * MXU_N], preferred_element_type=F32)


def _head_norm(y, c, hg_ref, m_ref):
    msq = jnp.dot((y * y).astype(BF16), m_ref[...], preferred_element_type=F32)
    return y * lax.rsqrt(msq + EPS) * hg_ref[:, c * MXU_N:(c + 1) * MXU_N]


def _rope_natural(y, cos, sa, sb):
    halves = []
    for p in range(MXU_N // LANES):
        z = y[:, p * LANES:(p + 1) * LANES]
        halves.append(z * cos + pltpu.roll(z, LANES - HEAD_DIM // 2, axis=1) * sa
                      + pltpu.roll(z, HEAD_DIM // 2, axis=1) * sb)
    return jnp.concatenate(halves, axis=1)


def _proj_sample_kernel(x_ref, g_ref, w_ref, hg_ref, cos_ref, sa_ref, sb_ref, m_ref, o_ref, *, n_rope):
    h = _rms(x_ref[...], g_ref[...]).astype(BF16)
    cos, sa, sb = cos_ref[...], sa_ref[...], sb_ref[...]
    for c in range(o_ref.shape[1] // MXU_N):
        y = _proj_dot(h, w_ref, c)
        if c * MXU_N < n_rope:
            y = _rope_natural(_head_norm(y, c, hg_ref, m_ref), cos, sa, sb)
        o_ref[:, c * MXU_N:(c + 1) * MXU_N] = y


def _proj_sample(x, g, w, w_index, hgain, tables, msum, *, n_rope):
    n, d = x.shape
    n_cols = w.shape[-1]
    full = lambda *shape: pl.BlockSpec(shape, lambda i: (0,) * len(shape))
    if w.ndim == 3:
        w_spec = pl.BlockSpec((None, d, n_cols), lambda i: (w_index, 0, 0))
    else:
        w_spec = full(d, n_cols)
    return pl.pallas_call(
        functools.partial(_proj_sample_kernel, n_rope=n_rope),
        grid=(1,),
        in_specs=[full(n, d), full(1, d), w_spec, full(1, n_rope),
                  full(n, LANES), full(n, LANES), full(n, LANES), full(MXU_N, MXU_N)],
        out_specs=full(n, n_cols),
        out_shape=jax.ShapeDtypeStruct((n, n_cols), F32),
        compiler_params=_cparams(1),
        name="proj_sample",
    )(x, g, w, hgain, *tables, msum)


def _proj_prompt_kernel(x_ref, g_ref, w_ref, hg_ref, cos_ref, sa_ref, sb_ref, m_ref, *refs,
                        tm, rope, tail_blocks, tail_start, tails_carried, zero_other_half):
    refs = refs[tails_carried:]
    outs, yscr = refs[:N_BRANCH], refs[-1]
    tails = refs[N_BRANCH:-1]
    j = pl.program_id(1)
    h = _rms(x_ref[...], g_ref[...]).astype(BF16)
    cos, sa, sb = cos_ref[...], sa_ref[...], sb_ref[...]
    per_branch = ATTN_WIDTH // MXU_N
    slabs = MXU_N // LANES
    n_chunks = PROJ_COLS // MXU_N
    y_next = _proj_dot(h, w_ref, 0)
    for c in range(n_chunks):
        g, cc = divmod(c, per_branch)
        y = y_next
        if c + 1 < n_chunks:
            y_next = _proj_dot(h, w_ref, c + 1)
        if rope:
            y = _rope_natural(_head_norm(y, c, hg_ref, m_ref), cos, sa, sb)
        dil = DILATIONS[g]
        if dil == 1:
            outs[g][0, 0, :, cc * MXU_N:(cc + 1) * MXU_N] = y.astype(BF16)
        if dil > 1 or tails:
            for p in range(slabs):
                yscr[c * slabs + p] = y[:, p * LANES:(p + 1) * LANES]
        if dil > 1:
            for r in range(dil):
                for p in range(slabs):
                    lo = cc * MXU_N + p * LANES
                    outs[g][0, r, :, lo:lo + LANES] = (
                        yscr[c * slabs + p, pl.ds(r, tm // dil, stride=dil), :].astype(BF16))
    for g in range(N_BRANCH if tails else 0):
        blk = tail_blocks[g]

        @pl.when(j >= tail_start[g])
        def _(g=g, blk=blk):
            for s in range(ATTN_WIDTH // LANES):
                slab = yscr[g * (ATTN_WIDTH // LANES) + s, tm - blk:, :]
                tails[g][0, 0, s * LANES:(s + 1) * LANES, :] = slab.T
            if zero_other_half:
                tails[g][0, 1] = jnp.zeros((ATTN_WIDTH, blk), F32)


def _proj_prompt(x, g, w, w_block, hgain, tables, msum, *, bsz, seq, tm, rope, tail_kv, tail_arrays):
    n, d = x.shape
    nt = seq // tm
    if w.ndim == 3:
        w_spec = pl.BlockSpec((None, d, PROJ_COLS), lambda b, j: (w_block, 0, 0), pipeline_mode=pl.Buffered(1))
    else:
        w_spec = pl.BlockSpec((d, PROJ_COLS), lambda b, j: (0, w_block), pipeline_mode=pl.Buffered(1))
    const = lambda *shape: pl.BlockSpec(shape, lambda b, j: (0,) * len(shape))
    tab = pl.BlockSpec((tm, LANES), lambda b, j: (j, 0))
    in_specs = [pl.BlockSpec((tm, d), lambda b, j: (b * nt + j, 0)), const(1, d), w_spec,
                const(1, PROJ_COLS), tab, tab, tab, const(MXU_N, MXU_N)]
    args = [x, g, w, hgain, *tables, msum]
    out_specs = [pl.BlockSpec((1, dil, tm // dil, ATTN_WIDTH), lambda b, j: (b, 0, j, 0)) for dil in DILATIONS]
    out_shape = [jax.ShapeDtypeStruct((bsz, dil, seq // dil, ATTN_WIDTH), BF16) for dil in DILATIONS]
    tail_blocks = tail_start = None
    aliases = {}
    if tail_kv is not None:
        create = tail_arrays is None
        assert create == (tail_kv == 0)
        keeps = [min(wd, seq) for wd in WINDOWS]
        tail_blocks = tuple(min(k, tm) for k in keeps)
        tail_start = tuple(nt - max(k // tm, 1) for k in keeps)
        for gi, (k, blk, j0) in enumerate(zip(keeps, tail_blocks, tail_start)):
            if create:
                spec = pl.BlockSpec((1, 2, ATTN_WIDTH, blk), lambda b, j, j0=j0: (b, 0, 0, jnp.maximum(j - j0, 0)))
            else:
                spec = pl.BlockSpec((1, 1, ATTN_WIDTH, blk),
                                    lambda b, j, j0=j0: (b, tail_kv, 0, jnp.maximum(j - j0, 0)))
                aliases[len(args)] = N_BRANCH + gi
                args.append(tail_arrays[gi])
                in_specs.append(pl.BlockSpec(memory_space=pl.ANY))
            out_specs.append(spec)
            out_shape.append(jax.ShapeDtypeStruct((bsz, 2, ATTN_WIDTH, k), F32))
    kern = functools.partial(_proj_prompt_kernel, tm=tm, rope=rope, tail_blocks=tail_blocks, tail_start=tail_start,
                             tails_carried=len(aliases), zero_other_half=tail_kv == 0)
    return pl.pallas_call(
        kern,
        grid=(bsz, nt),
        in_specs=in_specs,
        out_specs=out_specs,
        out_shape=out_shape,
        scratch_shapes=[pltpu.VMEM((PROJ_COLS // LANES, tm, LANES), F32)],
        input_output_aliases=aliases,
        compiler_params=_cparams(2),
        name="proj_prompt",
    )(*args)


ATTN_Q_BLOCKS = 4
LN2 = float(np.log(2.0))
LOG2E = float(1.0 / np.log(2.0))


def _lse_lane(head):
    return head if head % 2 == 0 else HEAD_DIM + head


def _attn_kernel(q_ref, kp_ref, kc_ref, vp_ref, vc_ref, bias_ref, o_ref, lse_ref):
    j = pl.program_id(1)
    lane = lax.broadcasted_iota(jnp.int32, (BLK, LANES), 1)
    first = lane < HEAD_DIM
    nt = (((1,), (1,)), ((), ()))
    bias_first = bias_ref[jnp.minimum(j, 1)]

    def keys_values(b, p, ref_prev, ref_cur):
        sl = slice(p * LANES, (p + 1) * LANES)
        prev = ref_prev[0, :, sl] if b == 0 else ref_cur[0, (b - 1) * BLK:b * BLK, sl]
        return jnp.concatenate([prev, ref_cur[0, b * BLK:(b + 1) * BLK, sl]], axis=0)

    def scores(b, p):
        q2 = q_ref[0, b * BLK:(b + 1) * BLK, p * LANES:(p + 1) * LANES]
        zero = jnp.zeros_like(q2)
        qq = jnp.concatenate([jnp.where(first, q2, zero), jnp.where(first, zero, q2)], axis=0)
        kk = keys_values(b, p, kp_ref, kc_ref)
        return lax.dot_general(qq, kk, nt, preferred_element_type=F32) + (bias_first if b == 0 else bias_ref[1])

    pairs = [(b, p) for b in range(q_ref.shape[1] // BLK) for p in range(N_HEADS // 2)]
    s_next = scores(*pairs[0])
    lse_all = None
    for i, (b, p) in enumerate(pairs):
        s = s_next
        if i + 1 < len(pairs):
            s_next = scores(*pairs[i + 1])
        m = jnp.max(s, axis=-1, keepdims=True)
        e = jnp.exp2(s - m).astype(BF16)
        vv = keys_values(b, p, vp_ref, vc_ref)
        v1 = jnp.concatenate([vv, jnp.ones_like(vv)], axis=1)
        pv = jnp.dot(e, v1, preferred_element_type=F32)
        rows, sl = slice(b * BLK, (b + 1) * BLK), slice(p * LANES, (p + 1) * LANES)
        num = jnp.where(first, pv[:BLK, :LANES], pv[BLK:, :LANES])
        den = jnp.where(first, pv[:BLK, LANES:], pv[BLK:, LANES:])
        o_ref[0, rows, sl] = (num / den).astype(o_ref.dtype)
        lse_pair = (jnp.where(first, m[:BLK], m[BLK:]) + jnp.log2(den)) * LN2
        here = (lane == _lse_lane(2 * p)) | (lane == _lse_lane(2 * p + 1))
        lse_all = jnp.where(here, lse_pair, jnp.zeros_like(lse_pair) if p == 0 else lse_all)
        if p == N_HEADS // 2 - 1:
            lse_ref[0, rows, :] = lse_all


def _attn(q, k, v, bias):
    ns, l, w = q.shape
    nq = min(ATTN_Q_BLOCKS, l // BLK)
    cur = lambda s, j: (s, j, 0)
    prev = lambda s, j: (s, jnp.maximum(j * nq - 1, 0), 0)
    big, small = (1, nq * BLK, w), (1, BLK, w)
    return pl.pallas_call(
        _attn_kernel,
        grid=(ns, l // (nq * BLK)),
        in_specs=[
            pl.BlockSpec(big, cur),
            pl.BlockSpec(small, prev), pl.BlockSpec(big, cur),
            pl.BlockSpec(small, prev), pl.BlockSpec(big, cur),
            pl.BlockSpec((2, 2 * BLK, 2 * BLK), lambda s, j: (0, 0, 0)),
        ],
        out_specs=[pl.BlockSpec(big, cur), pl.BlockSpec((1, nq * BLK, LANES), cur)],
        out_shape=[jax.ShapeDtypeStruct((ns, l, w), BF16), jax.ShapeDtypeStruct((ns, l, LANES), F32)],
        compiler_params=_cparams(2),
        name="attn_prompt",
    )(q, k, k, v, v, bias)


SAMPLE_CACHE_BLOCK_POSITIONS = 8192


def _attn_sample_kernel(q_ref, cache_ref, new_ref, bias_c_ref, o_ref, lse_ref, *cache_out, hist, dil, hb):
    n = pl.program_id(0)
    slot = (n % (LANES // 8)) * 8
    lane = lax.broadcasted_iota(jnp.int32, (8, LANES), 1)
    t_q = lax.broadcasted_iota(jnp.int32, (8, LANES), 0)
    diff = t_q - (lane - slot)
    ok = (diff >= 0) & (diff <= t_q) & ((diff & (dil - 1)) == 0)
    bias_n = jnp.where(ok, 0.0, NEG)
    bias_c = bias_c_ref[...]
    lane_c = lax.broadcasted_iota(jnp.int32, (HEAD_DIM, LANES), 1)
    nt = (((1,), (1,)), ((), ()))
    s_cs = [jnp.dot(q_ref[0, h], cache_ref[0, 0, h], preferred_element_type=F32) + bias_c for h in range(hb)]
    s_ns = [jnp.dot(q_ref[0, h], new_ref[0, h], preferred_element_type=F32) + bias_n for h in range(hb)]
    ms = [jnp.maximum(jnp.max(s_c, axis=-1, keepdims=True), jnp.max(s_n, axis=-1, keepdims=True))
          for s_c, s_n in zip(s_cs, s_ns)]
    e_cs = [jnp.exp(s_c - m) for s_c, m in zip(s_cs, ms)]
    e_ns = [jnp.exp(s_n - m) for s_n, m in zip(s_ns, ms)]
    ls = [jnp.sum(e_c, axis=-1, keepdims=True) + jnp.sum(e_n, axis=-1, keepdims=True)
          for e_c, e_n in zip(e_cs, e_ns)]
    for h in range(hb):
        o = (lax.dot_general(e_cs[h], cache_ref[0, 1, h], nt, preferred_element_type=F32)
             + lax.dot_general(e_ns[h], new_ref[1, h], nt, preferred_element_type=F32))
        o_ref[0, h] = o / ls[h]
        lse_ref[0, h] = jnp.broadcast_to(ms[h] + jnp.log(ls[h]), (8, HEAD_DIM))
    for out_ref in cache_out:
        for h in range(hb):
            for kv in range(2):
                shifted = pltpu.roll(cache_ref[0, kv, h], hist - 8, axis=1)
                if hist > LANES:
                    out_ref[0, kv, h, :, :hist - LANES] = shifted[:, :hist - LANES]
                moved = pltpu.roll(new_ref[kv, h], (LANES - 8) - slot, axis=1)
                out_ref[0, kv, h, :, hist - LANES:] = jnp.where(
                    lane_c >= LANES - 8, moved, shifted[:, hist - LANES:])


def _attn_sample(q, cache, new_t, bias_c, branch, *, dil, write_cache):
    nreq = q.shape[1]
    hist = cache.shape[-1]
    hb = min(N_HEADS, SAMPLE_CACHE_BLOCK_POSITIONS // hist)
    o_spec = pl.BlockSpec((1, hb, 8, HEAD_DIM), lambda n, h: (n, h, 0, 0))
    out_specs = [o_spec, o_spec]
    out_shape = [jax.ShapeDtypeStruct((nreq, N_HEADS, 8, HEAD_DIM), F32)] * 2
    if write_cache:
        out_specs.append(pl.BlockSpec((1, 2, hb, HEAD_DIM, hist), lambda n, h: (n, 0, h, 0, 0)))
        out_shape.append(jax.ShapeDtypeStruct(cache.shape, F32))
    return pl.pallas_call(
        functools.partial(_attn_sample_kernel, hist=hist, dil=dil, hb=hb),
        grid=(nreq, N_HEADS // hb),
        in_specs=[
            pl.BlockSpec((None, 1, hb, 8, HEAD_DIM), lambda n, h: (branch, n, h, 0, 0)),
            pl.BlockSpec((1, 2, hb, HEAD_DIM, hist), lambda n, h: (n, 0, h, 0, 0)),
            pl.BlockSpec((2, None, hb, HEAD_DIM, LANES), lambda n, h: (0, branch, h, 0, n // (LANES // 8))),
            pl.BlockSpec((8, hist), lambda n, h: (0, 0)),
        ],
        out_specs=out_specs,
        out_shape=out_shape,
        compiler_params=_cparams(2),
        name="attn_sample",
    )(q, cache, new_t, bias_c)


def _merge_sample_kernel(x_ref, o0, o1, o2, l0, l1, l2, w_ref, out_ref):
    ls = [l0[...], l1[...], l2[...]]
    m = jnp.maximum(jnp.maximum(ls[0], ls[1]), ls[2])
    es = [jnp.exp(l - m) for l in ls]
    den = es[0] + es[1] + es[2]
    num = es[0] * o0[...] + es[1] * o1[...] + es[2] * o2[...]
    merged = (num / den).astype(BF16)
    out_ref[...] = x_ref[...] + jnp.dot(merged, w_ref[...], preferred_element_type=F32)


def _merge_sample(x, outs, lses, w_o, layer):
    n, d = x.shape
    tile = pl.BlockSpec((n, d), lambda i: (0, 0))
    return pl.pallas_call(
        _merge_sample_kernel,
        grid=(1,),
        in_specs=[tile] * 7 + [pl.BlockSpec((None, d, d), lambda i: (layer, 0, 0))],
        out_specs=tile,
        out_shape=jax.ShapeDtypeStruct((n, d), F32),
        compiler_params=_cparams(1),
        name="merge_sample",
    )(x, *outs, *lses, w_o)


WEIGHT_PART_LANES = N_HEADS


def _merge_prompt_kernel(x_ref, o0_ref, o1_ref, o2_ref, l0_ref, l1_ref, l2_ref, e_ref, w_ref, out_ref,
                         oscr1, oscr2, lscr1, lscr2, *, tm):
    o_refs, l_refs = (o0_ref, o1_ref, o2_ref), (l0_ref, l1_ref, l2_ref)
    o_scr, l_scr = (None, oscr1, oscr2), (None, lscr1, lscr2)
    os_, ls_ = [], []
    for g, dil in enumerate(DILATIONS):
        if dil == 1:
            os_.append(o_refs[g][0, 0].astype(F32))
            ls_.append(l_refs[g][0, 0])
        else:
            for r in range(dil):
                rows = pl.ds(r, tm // dil, stride=dil)
                o_r = o_refs[g][0, r].astype(F32)
                for c in range(ATTN_WIDTH // LANES):
                    o_scr[g][c, rows, :] = o_r[:, c * LANES:(c + 1) * LANES]
                l_scr[g][rows, :] = l_refs[g][0, r]
            os_.append(jnp.concatenate([o_scr[g][c] for c in range(ATTN_WIDTH // LANES)], axis=1))
            ls_.append(l_scr[g][...])
    m = jnp.maximum(jnp.maximum(ls_[0], ls_[1]), ls_[2])
    es = [jnp.exp(l - m) for l in ls_]
    den = es[0] + es[1] + es[2]
    lane = lax.broadcasted_iota(jnp.int32, (1, LANES), 1)
    head_lane = functools.reduce(jnp.logical_or, [lane == _lse_lane(h) for h in range(N_HEADS)])
    packed = None
    for g in range(N_BRANCH):
        w = jnp.where(head_lane, es[g] / den, 0.0)
        hi = w.astype(BF16).astype(F32)
        lo = (w - hi).astype(BF16).astype(F32)
        for part, val in ((2 * g, hi), (2 * g + 1, lo)):
            if part:
                val = pltpu.roll(val, part * WEIGHT_PART_LANES, axis=1)
            packed = val if packed is None else packed + val
    spread = jnp.dot(packed.astype(BF16), e_ref[...], preferred_element_type=F32)
    merged = sum(spread[:, g * ATTN_WIDTH:(g + 1) * ATTN_WIDTH] * os_[g] for g in range(N_BRANCH))
    out_ref[...] = x_ref[...] + jnp.dot(merged.astype(BF16), w_ref[...], preferred_element_type=F32)


def _merge_prompt(x, outs, lses, spread_mat, w_o, layer, *, bsz, seq, tm):
    n, d = x.shape
    nt = seq // tm
    tile = pl.BlockSpec((tm, d), lambda b, j: (b * nt + j, 0))
    cls = lambda width: [pl.BlockSpec((1, dil, tm // dil, width), lambda b, j: (b, 0, j, 0))
                         for dil in DILATIONS]
    return pl.pallas_call(
        functools.partial(_merge_prompt_kernel, tm=tm),
        grid=(bsz, nt),
        in_specs=[tile, *cls(ATTN_WIDTH), *cls(LANES),
                  pl.BlockSpec((LANES, N_BRANCH * ATTN_WIDTH), lambda b, j: (0, 0)),
                  pl.BlockSpec((None, d, d), lambda b, j: (layer, 0, 0))],
        out_specs=tile,
        out_shape=jax.ShapeDtypeStruct((n, d), F32),
        scratch_shapes=[pltpu.VMEM((ATTN_WIDTH // LANES, tm, LANES), F32),
                        pltpu.VMEM((ATTN_WIDTH // LANES, tm, LANES), F32),
                        pltpu.VMEM((tm, LANES), F32), pltpu.VMEM((tm, LANES), F32)],
        compiler_params=_cparams(2),
        name="merge_prompt",
    )(x, *outs, *lses, spread_mat, w_o)


def _rope_tables(pos):
    inv = jnp.power(ROPE_THETA, -jnp.arange(0, HEAD_DIM, 2, dtype=F32) / HEAD_DIM)
    ang = pos.astype(F32)[:, None] * inv[None, :]
    ang = jnp.concatenate([ang, ang], axis=-1)
    cos, sin = jnp.cos(ang), jnp.sin(ang)
    first = jnp.arange(HEAD_DIM) < HEAD_DIM // 2
    sa = jnp.where(first, -sin, 0.0)
    sb = jnp.where(first, 0.0, sin)
    tile2 = lambda a: jnp.concatenate([a, a], axis=-1)
    return tile2(cos), tile2(sa), tile2(sb)


def _head_mean_matrix():
    head = np.arange(MXU_N) // HEAD_DIM
    return jnp.asarray((head[:, None] == head[None, :]) / HEAD_DIM, dtype=BF16)


def _prompt_bias():
    qi = np.arange(BLK)[:, None]
    ki = np.arange(2 * BLK)[None, :]
    rel = qi + BLK - ki
    band = (rel >= 0) & (rel <= BLK)
    first = band & (ki >= BLK)
    both = np.stack([np.tile(first, (2, 1)), np.tile(band, (2, 1))])
    return jnp.asarray(np.where(both, 0.0, NEG), dtype=F32)


def _sample_bias(hist, dil):
    t = np.arange(8)[:, None]
    l = np.arange(hist)[None, :]
    ok_c = (l >= t) & ((l - t) % dil == 0)
    return jnp.asarray(np.where(ok_c, 0.0, NEG), dtype=F32)


def _weight_spread_matrix():
    hit = np.zeros((LANES, N_BRANCH * ATTN_WIDTH), np.float32)
    used = set()
    for part in range(2 * N_BRANCH):
        for h in range(N_HEADS):
            lane = (_lse_lane(h) + part * WEIGHT_PART_LANES) % LANES
            assert lane not in used
            used.add(lane)
            col = (part // 2) * ATTN_WIDTH + h * HEAD_DIM
            hit[lane, col:col + HEAD_DIM] = 1.0
    return jnp.asarray(hit, dtype=BF16)


def kernel(x_prompt, x_sample, state_pool, cache_kv_w128, cache_kv_w512, cache_kv_w2048,
           a_norm, pool_w, pool_scale, kv_norm, w_kv, k_norm, b_norm, w_q, q_norm, w_o,
           ffn_norm, w_gate, w_up, w_down):
    bsz, seq, d = x_prompt.shape
    nreq, t_new, _ = x_sample.shape
    caches = (cache_kv_w128, cache_kv_w512, cache_kv_w2048)
    n_p, n_s = bsz * seq, nreq * t_new

    pool_w_b = pool_w.astype(BF16)
    w_kv_b, w_q_b, w_o_b = w_kv.astype(BF16), w_q.astype(BF16), w_o.astype(BF16)
    wg_b, wu_b, wd_b = w_gate.astype(BF16), w_up.astype(BF16), w_down.astype(BF16)

    tab_p = _rope_tables(jnp.arange(seq))
    tab_s = tuple(jnp.tile(a, (nreq, 1)) for a in _rope_tables(PAST_LEN + jnp.arange(t_new)))
    msum = _head_mean_matrix()
    k_gain = jnp.tile(k_norm[:, None, :], (1, N_HEADS, 1)).reshape(1, PROJ_COLS)
    bias_p = _prompt_bias()

    xp, xs = x_prompt, x_sample
    pool_p, pool_s = [], []
    for i in range(N_A_LAYERS):
        layer_w = (a_norm[i:i + 1], pool_w_b[i], pool_scale[i:i + 1], ffn_norm, wg_b, wu_b, wd_b, i)
        zeros_hist = jnp.zeros((bsz, HALO, d), F32)
        xp, hp = _pool_ffn_layer(xp, zeros_hist, *layer_w, nb=1, tm=TM_PROMPT, full_hist=False)
        hist_s = jnp.pad(state_pool[:, i], ((0, 0), (1, 0), (0, 0)))
        xs, hs = _pool_ffn_layer(xs, hist_s, *layer_w, nb=nreq, tm=t_new, full_hist=True)
        pool_p.append(hp[:, 1:])
        pool_s.append(hs[:, 1:])

    xp, xs = xp.reshape(n_p, d), xs.reshape(n_s, d)
    kv_norm2 = kv_norm.reshape(1, d)
    prompt = dict(bsz=bsz, seq=seq, tm=TM_PROMPT)
    k_out = _proj_prompt(xp, kv_norm2, w_kv_b, 0, k_gain, tab_p, msum, rope=True,
                         tail_kv=0, tail_arrays=None, **prompt)
    v_out = _proj_prompt(xp, kv_norm2, w_kv_b, 1, k_gain, tab_p, msum, rope=False,
                         tail_kv=1, tail_arrays=k_out[N_BRANCH:], **prompt)
    k_cls, v_cls = k_out[:N_BRANCH], v_out[:N_BRANCH]
    kv_new_p = [t.reshape(bsz, 2, N_HEADS, HEAD_DIM, t.shape[-1]).transpose(0, 4, 1, 2, 3)
                for t in v_out[N_BRANCH:]]
    k_cls = [a.reshape(bsz * dil, seq // dil, ATTN_WIDTH) for a, dil in zip(k_cls, DILATIONS)]
    v_cls = [a.reshape(bsz * dil, seq // dil, ATTN_WIDTH) for a, dil in zip(v_cls, DILATIONS)]

    kv_s = _proj_sample(xs, kv_norm2, w_kv_b, 0, k_gain, tab_s, msum, n_rope=PROJ_COLS)
    caches_t = [c.transpose(0, 2, 3, 4, 1) for c in caches]
    new_t = kv_s.T.reshape(2, N_BRANCH, N_HEADS, HEAD_DIM, n_s)
    sample_bias = [_sample_bias(c.shape[1], dil) for c, dil in zip(caches, DILATIONS)]
    spread_mat = _weight_spread_matrix()

    kv_sample_out = [None] * N_BRANCH
    for j in range(N_B_LAYERS):
        layer = N_A_LAYERS + j
        q_gain = jnp.tile(q_norm[j][:, None, :], (1, N_HEADS, 1)).reshape(1, PROJ_COLS)
        q_gain = q_gain * (HEAD_DIM ** -0.5)
        bn = b_norm[j:j + 1]
        q_cls = _proj_prompt(xp, bn, w_q_b, j, q_gain * LOG2E, tab_p, msum, rope=True,
                             tail_kv=None, tail_arrays=None, **prompt)
        q_s = _proj_sample(xs, bn, w_q_b, j, q_gain, tab_s, msum, n_rope=PROJ_COLS)
        q_s5 = q_s.reshape(nreq, t_new, N_BRANCH, N_HEADS, HEAD_DIM).transpose(2, 0, 3, 1, 4)

        outs_p, lses_p, outs_s, lses_s = [], [], [], []
        for g, dil in enumerate(DILATIONS):
            q_c = q_cls[g].reshape(bsz * dil, seq // dil, ATTN_WIDTH)
            o_c, lse_c = _attn(q_c, k_cls[g], v_cls[g], bias_p)
            outs_p.append(o_c.reshape(bsz, dil, seq // dil, ATTN_WIDTH))
            lses_p.append(lse_c.reshape(bsz, dil, seq // dil, LANES))

            res = _attn_sample(q_s5, caches_t[g], new_t, sample_bias[g], g, dil=dil, write_cache=(j == 0))
            outs_s.append(res[0].transpose(0, 2, 1, 3).reshape(n_s, ATTN_WIDTH))
            lses_s.append(res[1].transpose(0, 2, 1, 3).reshape(n_s, ATTN_WIDTH))
            if j == 0:
                kv_sample_out[g] = res[2].transpose(0, 4, 1, 2, 3)

        xp = _merge_prompt(xp, outs_p, lses_p, spread_mat, w_o_b, j, **prompt)
        xs = _merge_sample(xs, outs_s, lses_s, w_o_b, j)
        xp = _ffn(xp, ffn_norm, wg_b, wu_b, wd_b, layer, tm=TM_PROMPT)
        xs = _ffn(xs, ffn_norm, wg_b, wu_b, wd_b, layer, tm=n_s)

    return (xp.reshape(bsz, seq, d), xs.reshape(nreq, t_new, d),
            jnp.stack(pool_p, axis=1), jnp.stack(pool_s, axis=1),
            kv_new_p[0], kv_sample_out[0], kv_new_p[1], kv_sample_out[1],
            kv_new_p[2], kv_sample_out[2])
```

```python
import functools

import jax
import jax.numpy as jnp
import numpy as np
from jax import lax
from jax.experimental import pallas as pl
from jax.experimental.pallas import tpu as pltpu

F32 = jnp.float32
BF16 = jnp.bfloat16

D_MODEL = 1024
N_A_LAYERS = 2
N_B_LAYERS = 2
POOL_WINDOWS = (2, 4, 8, 16)
POOL_GROUP = 256
HALO = 16
POOL_HIST = 15
WINDOWS = (128, 512, 2048)
DILATIONS = (1, 4, 16)
N_BRANCH = 3
HEAD_DIM = 64
N_HEADS = 16
ATTN_WIDTH = N_HEADS * HEAD_DIM
D_FF = 2816
ROPE_THETA = 10000.0
EPS = 1e-6
PAST_LEN = 8192
BLK = 128
NEG = -1e30

LANES = 128
MXU_N = 256
VMEM_LIMIT = 56 * 1024 * 1024
TM_PROMPT = 512


def _cparams(n_axes):
    return pltpu.CompilerParams(dimension_semantics=("arbitrary",) * n_axes,
                                vmem_limit_bytes=VMEM_LIMIT)


def _rms(x, g):
    return x * lax.rsqrt(jnp.mean(x * x, axis=-1, keepdims=True) + EPS) * g


FF_CHUNK = MXU_N


def _swiglu_residual(x, g_ref, wg_ref, wu_ref, wd_ref):
    h = _rms(x, g_ref[...]).astype(BF16)
    acc = x
    for c in range(D_FF // FF_CHUNK):
        sl = slice(c * FF_CHUNK, (c + 1) * FF_CHUNK)
        a = jnp.dot(h, wg_ref[:, sl].astype(BF16), preferred_element_type=F32)
        b = jnp.dot(h, wu_ref[:, sl].astype(BF16), preferred_element_type=F32)
        t = (a * jax.nn.sigmoid(a) * b).astype(BF16)
        acc = acc + jnp.dot(t, wd_ref[sl, :].astype(BF16), preferred_element_type=F32)
    return acc


def _ffn_weight_specs(layer):
    idx = lambda *_: (layer, 0, 0)
    resident = dict(pipeline_mode=pl.Buffered(1))
    return [pl.BlockSpec((1, D_MODEL), lambda *_: (0, 0)),
            pl.BlockSpec((None, D_MODEL, D_FF), idx, **resident),
            pl.BlockSpec((None, D_MODEL, D_FF), idx, **resident),
            pl.BlockSpec((None, D_FF, D_MODEL), idx, **resident)]


def _pool_ffn_kernel(x_ref, xh_ref, hist_ref, g_ref, w_ref, sc_ref, fg_ref, wg_ref, wu_ref, wd_ref,
                     o_ref, ho_ref, *, tm, full_hist):
    j = pl.program_id(1)
    nb = x_ref.shape[0]
    x = x_ref[...]
    g = g_ref[...]
    u = _rms(x, g)
    hu = _rms(xh_ref[...], g)
    h = jnp.where(j == 0, hist_ref[...], hu)
    ext = jnp.concatenate([h, u], axis=1)
    ho_ref[...] = ext[:, tm:tm + HALO]
    s = ext
    sums = {}
    for step in (1, 2, 4, 8):
        s = s + pltpu.roll(s, step, axis=1)
        sums[2 * step] = s
    row = j * tm + lax.broadcasted_iota(jnp.int32, (1, tm, 1), 1)
    mixed = []
    for gi, win in enumerate(POOL_WINDOWS):
        lo, hi = gi * POOL_GROUP, (gi + 1) * POOL_GROUP
        sw = sums[win][:, HALO:, lo:hi]
        if full_hist:
            cnt = jnp.float32(win)
        else:
            cnt = jnp.minimum(row + 1, win).astype(F32)
        d = (sw / cnt - u[:, :, lo:hi]).reshape(nb * tm, POOL_GROUP)
        y = jnp.dot(d.astype(BF16), w_ref[gi], preferred_element_type=F32)
        mixed.append(x[:, :, lo:hi].reshape(nb * tm, POOL_GROUP) + y * sc_ref[:, lo:hi])
    x1 = jnp.concatenate(mixed, axis=1)
    o_ref[...] = _swiglu_residual(x1, fg_ref, wg_ref, wu_ref, wd_ref).reshape(nb, tm, D_MODEL)


def _pool_ffn_layer(x, hist, a_norm, w_pool, scale, ffn_gain, wg, wu, wd, layer, *, nb, tm, full_hist):
    b, s, d = x.shape
    nt = s // tm
    if full_hist:
        halo_arr, halo_map = hist, (lambda bi, j: (bi, 0, 0))
    else:
        r = tm // HALO
        halo_arr, halo_map = x, (lambda bi, j: (bi, jnp.maximum(j * r - 1, 0), 0))
    return pl.pallas_call(
        functools.partial(_pool_ffn_kernel, tm=tm, full_hist=full_hist),
        grid=(b // nb, nt),
        in_specs=[
            pl.BlockSpec((nb, tm, d), lambda bi, j: (bi, j, 0)),
            pl.BlockSpec((nb, HALO, d), halo_map),
            pl.BlockSpec((nb, HALO, d), lambda bi, j: (bi, 0, 0)),
            pl.BlockSpec((1, d), lambda bi, j: (0, 0)),
            pl.BlockSpec((4, POOL_GROUP, POOL_GROUP), lambda bi, j: (0, 0, 0)),
            pl.BlockSpec((1, d), lambda bi, j: (0, 0)),
            *_ffn_weight_specs(layer),
        ],
        out_specs=[
            pl.BlockSpec((nb, tm, d), lambda bi, j: (bi, j, 0)),
            pl.BlockSpec((nb, HALO, d), lambda bi, j: (bi, 0, 0)),
        ],
        out_shape=[jax.ShapeDtypeStruct((b, s, d), F32),
                   jax.ShapeDtypeStruct((b, HALO, d), F32)],
        compiler_params=_cparams(2),
        name="pool_ffn_layer",
    )(x, halo_arr, hist, a_norm, w_pool, scale, ffn_gain[layer:layer + 1], wg, wu, wd)


def _ffn_kernel(x_ref, g_ref, wg_ref, wu_ref, wd_ref, o_ref):
    o_ref[...] = _swiglu_residual(x_ref[...], g_ref, wg_ref, wu_ref, wd_ref)


def _ffn(x, g, wg, wu, wd, layer, *, tm):
    n, d = x.shape
    return pl.pallas_call(
        _ffn_kernel,
        grid=(n // tm,),
        in_specs=[pl.BlockSpec((tm, d), lambda i: (i, 0)), *_ffn_weight_specs(layer)],
        out_specs=pl.BlockSpec((tm, d), lambda i: (i, 0)),
        out_shape=jax.ShapeDtypeStruct((n, d), F32),
        compiler_params=_cparams(1),
        name="ffn",
    )(x, g[layer:layer + 1], wg, wu, wd)


PROJ_COLS = N_BRANCH * ATTN_WIDTH


def _proj_dot(h, w_ref, c):
    return jnp.dot(h, w_ref[:, c * MXU_N:(c + 1) * MXU_N], preferred_element_type=F32)


def _head_norm(y, c, hg_ref, m_ref):
    msq = jnp.dot((y * y).astype(BF16), m_ref[...], preferred_element_type=F32)
    return y * lax.rsqrt(msq + EPS) * hg_ref[:, c * MXU_N:(c + 1) * MXU_N]


def _rope_natural(y, cos, sa, sb):
    halves = []
    for p in range(MXU_N // LANES):
        z = y[:, p * LANES:(p + 1) * LANES]
        halves.append(z * cos + pltpu.roll(z, LANES - HEAD_DIM // 2, axis=1) * sa
                      + pltpu.roll(z, HEAD_DIM // 2, axis=1) * sb)
    return jnp.concatenate(halves, axis=1)


def _proj_sample_kernel(x_ref, g_ref, w_ref, hg_ref, cos_ref, sa_ref, sb_ref, m_ref, o_ref, *, n_rope):
    h = _rms(x_ref[...], g_ref[...]).astype(BF16)
    cos, sa, sb = cos_ref[...], sa_ref[...], sb_ref[...]
    for c in range(o_ref.shape[1] // MXU_N):
        y = _proj_dot(h, w_ref, c)
        if c * MXU_N < n_rope:
            y = _rope_natural(_head_norm(y, c, hg_ref, m_ref), cos, sa, sb)
        o_ref[:, c * MXU_N:(c + 1) * MXU_N] = y


def _proj_sample(x, g, w, w_index, hgain, tables, msum, *, n_rope):
    n, d = x.shape
    n_cols = w.shape[-1]
    full = lambda *shape: pl.BlockSpec(shape, lambda i: (0,) * len(shape))
    if w.ndim == 3:
        w_spec = pl.BlockSpec((None, d, n_cols), lambda i: (w_index, 0, 0))
    else:
        w_spec = full(d, n_cols)
    return pl.pallas_call(
        functools.partial(_proj_sample_kernel, n_rope=n_rope),
        grid=(1,),
        in_specs=[full(n, d), full(1, d), w_spec, full(1, n_rope),
                  full(n, LANES), full(n, LANES), full(n, LANES), full(MXU_N, MXU_N)],
        out_specs=full(n, n_cols),
        out_shape=jax.ShapeDtypeStruct((n, n_cols), F32),
        compiler_params=_cparams(1),
        name="proj_sample",
    )(x, g, w, hgain, *tables, msum)


def _proj_prompt_kernel(x_ref, g_ref, w_ref, hg_ref, cos_ref, sa_ref, sb_ref, m_ref, *refs,
                        tm, rope, tail_blocks, tail_start, tails_carried, zero_other_half):
    refs = refs[tails_carried:]
    outs, yscr = refs[:N_BRANCH], refs[-1]
    tails = refs[N_BRANCH:-1]
    j = pl.program_id(1)
    h = _rms(x_ref[...], g_ref[...]).astype(BF16)
    cos, sa, sb = cos_ref[...], sa_ref[...], sb_ref[...]
    per_branch = ATTN_WIDTH // MXU_N
    slabs = MXU_N // LANES
    n_chunks = PROJ_COLS // MXU_N
    y_next = _proj_dot(h, w_ref, 0)
    for c in range(n_chunks):
        g, cc = divmod(c, per_branch)
        y = y_next
        if c + 1 < n_chunks:
            y_next = _proj_dot(h, w_ref, c + 1)
        if rope:
            y = _rope_natural(_head_norm(y, c, hg_ref, m_ref), cos, sa, sb)
        dil = DILATIONS[g]
        if dil == 1:
            outs[g][0, 0, :, cc * MXU_N:(cc + 1) * MXU_N] = y.astype(BF16)
        if dil > 1 or tails:
            for p in range(slabs):
                yscr[c * slabs + p] = y[:, p * LANES:(p + 1) * LANES]
        if dil > 1:
            for r in range(dil):
                for p in range(slabs):
                    lo = cc * MXU_N + p * LANES
                    outs[g][0, r, :, lo:lo + LANES] = (
                        yscr[c * slabs + p, pl.ds(r, tm // dil, stride=dil), :].astype(BF16))
    for g in range(N_BRANCH if tails else 0):
        blk = tail_blocks[g]

        @pl.when(j >= tail_start[g])
        def _(g=g, blk=blk):
            for s in range(ATTN_WIDTH // LANES):
                slab = yscr[g * (ATTN_WIDTH // LANES) + s, tm - blk:, :]
                tails[g][0, 0, s * LANES:(s + 1) * LANES, :] = slab.T
            if zero_other_half:
                tails[g][0, 1] = jnp.zeros((ATTN_WIDTH, blk), F32)


def _proj_prompt(x, g, w, w_block, hgain, tables, msum, *, bsz, seq, tm, rope, tail_kv, tail_arrays):
    n, d = x.shape
    nt = seq // tm
    if w.ndim == 3:
        w_spec = pl.BlockSpec((None, d, PROJ_COLS), lambda b, j: (w_block, 0, 0), pipeline_mode=pl.Buffered(1))
    else:
        w_spec = pl.BlockSpec((d, PROJ_COLS), lambda b, j: (0, w_block), pipeline_mode=pl.Buffered(1))
    const = lambda *shape: pl.BlockSpec(shape, lambda b, j: (0,) * len(shape))
    tab = pl.BlockSpec((tm, LANES), lambda b, j: (j, 0))
    in_specs = [pl.BlockSpec((tm, d), lambda b, j: (b * nt + j, 0)), const(1, d), w_spec,
                const(1, PROJ_COLS), tab, tab, tab, const(MXU_N, MXU_N)]
    args = [x, g, w, hgain, *tables, msum]
    out_specs = [pl.BlockSpec((1, dil, tm // dil, ATTN_WIDTH), lambda b, j: (b, 0, j, 0)) for dil in DILATIONS]
    out_shape = [jax.ShapeDtypeStruct((bsz, dil, seq // dil, ATTN_WIDTH), BF16) for dil in DILATIONS]
    tail_blocks = tail_start = None
    aliases = {}
    if tail_kv is not None:
        create = tail_arrays is None
        assert create == (tail_kv == 0)
        keeps = [min(wd, seq) for wd in WINDOWS]
        tail_blocks = tuple(min(k, tm) for k in keeps)
        tail_start = tuple(nt - max(k // tm, 1) for k in keeps)
        for gi, (k, blk, j0) in enumerate(zip(keeps, tail_blocks, tail_start)):
            if create:
                spec = pl.BlockSpec((1, 2, ATTN_WIDTH, blk), lambda b, j, j0=j0: (b, 0, 0, jnp.maximum(j - j0, 0)))
            else:
                spec = pl.BlockSpec((1, 1, ATTN_WIDTH, blk),
                                    lambda b, j, j0=j0: (b, tail_kv, 0, jnp.maximum(j - j0, 0)))
                aliases[len(args)] = N_BRANCH + gi
                args.append(tail_arrays[gi])
                in_specs.append(pl.BlockSpec(memory_space=pl.ANY))
            out_specs.append(spec)
            out_shape.append(jax.ShapeDtypeStruct((bsz, 2, ATTN_WIDTH, k), F32))
    kern = functools.partial(_proj_prompt_kernel, tm=tm, rope=rope, tail_blocks=tail_blocks, tail_start=tail_start,
                             tails_carried=len(aliases), zero_other_half=tail_kv == 0)
    return pl.pallas_call(
        kern,
        grid=(bsz, nt),
        in_specs=in_specs,
        out_specs=out_specs,
        out_shape=out_shape,
        scratch_shapes=[pltpu.VMEM((PROJ_COLS // LANES, tm, LANES), F32)],
        input_output_aliases=aliases,
        compiler_params=_cparams(2),
        name="proj_prompt",
    )(*args)


ATTN_Q_BLOCKS = 4
LN2 = float(np.log(2.0))
LOG2E = float(1.0 / np.log(2.0))


def _lse_lane(head):
    return head if head % 2 == 0 else HEAD_DIM + head


def _attn_kernel(q_ref, kp_ref, kc_ref, vp_ref, vc_ref, bias_ref, o_ref, lse_ref):
    j = pl.program_id(1)
    lane = lax.broadcasted_iota(jnp.int32, (BLK, LANES), 1)
    first = lane < HEAD_DIM
    nt = (((1,), (1,)), ((), ()))
    bias_first = bias_ref[jnp.minimum(j, 1)]

    def keys_values(s, b, p, ref_prev, ref_cur):
        sl = slice(p * LANES, (p + 1) * LANES)
        prev = ref_prev[s, :, sl] if b == 0 else ref_cur[s, (b - 1) * BLK:b * BLK, sl]
        return jnp.concatenate([prev, ref_cur[s, b * BLK:(b + 1) * BLK, sl]], axis=0)

    def scores(s, b, p):
        q2 = q_ref[s, b * BLK:(b + 1) * BLK, p * LANES:(p + 1) * LANES]
        zero = jnp.zeros_like(q2)
        qq = jnp.concatenate([jnp.where(first, q2, zero), jnp.where(first, zero, q2)], axis=0)
        kk = keys_values(s, b, p, kp_ref, kc_ref)
        return lax.dot_general(qq, kk, nt, preferred_element_type=F32) + (bias_first if b == 0 else bias_ref[1])

    n_seq, n_blk = q_ref.shape[0], q_ref.shape[1] // BLK
    pairs = [(s, b, p) for s in range(n_seq) for b in range(n_blk) for p in range(N_HEADS // 2)]
    s_next = scores(*pairs[0])
    lse_all = None
    for i, (sq, b, p) in enumerate(pairs):
        s = s_next
        if i + 1 < len(pairs):
            s_next = scores(*pairs[i + 1])
        m = jnp.max(s, axis=-1, keepdims=True)
        e = jnp.exp2(s - m).astype(BF16)
        vv = keys_values(sq, b, p, vp_ref, vc_ref)
        v1 = jnp.concatenate([vv, jnp.ones_like(vv)], axis=1)
        pv = jnp.dot(e, v1, preferred_element_type=F32)
        rows, sl = slice(b * BLK, (b + 1) * BLK), slice(p * LANES, (p + 1) * LANES)
        num = jnp.where(first, pv[:BLK, :LANES], pv[BLK:, :LANES])
        den = jnp.where(first, pv[:BLK, LANES:], pv[BLK:, LANES:])
        o_ref[sq, rows, sl] = (num / den).astype(o_ref.dtype)
        lse_pair = (jnp.where(first, m[:BLK], m[BLK:]) + jnp.log2(den)) * LN2
        here = (lane == _lse_lane(2 * p)) | (lane == _lse_lane(2 * p + 1))
        lse_all = jnp.where(here, lse_pair, jnp.zeros_like(lse_pair) if p == 0 else lse_all)
        if p == N_HEADS // 2 - 1:
            lse_ref[sq, rows, :] = lse_all


def _attn(q, k, v, bias):
    ns, l, w = q.shape
    nq = min(ATTN_Q_BLOCKS, l // BLK)
    nsq = ATTN_Q_BLOCKS // nq
    cur = lambda s, j: (s, j, 0)
    prev = lambda s, j: (s, jnp.maximum(j * nq - 1, 0), 0)
    big, small = (nsq, nq * BLK, w), (nsq, BLK, w)
    return pl.pallas_call(
        _attn_kernel,
        grid=(ns // nsq, l // (nq * BLK)),
        in_specs=[
            pl.BlockSpec(big, cur),
            pl.BlockSpec(small, prev), pl.BlockSpec(big, cur),
            pl.BlockSpec(small, prev), pl.BlockSpec(big, cur),
            pl.BlockSpec((2, 2 * BLK, 2 * BLK), lambda s, j: (0, 0, 0)),
        ],
        out_specs=[pl.BlockSpec(big, cur), pl.BlockSpec((nsq, nq * BLK, LANES), cur)],
        out_shape=[jax.ShapeDtypeStruct((ns, l, w), BF16), jax.ShapeDtypeStruct((ns, l, LANES), F32)],
        compiler_params=_cparams(2),
        name="attn_prompt",
    )(q, k, k, v, v, bias)


SAMPLE_CACHE_BLOCK_POSITIONS = 16384


REQUESTS_PER_NEW_BLOCK = LANES // 8


def _attn_sample_kernel(q_ref, cache_ref, new_ref, bias_c_ref, o_ref, lse_ref, *cache_out, hist, dil):
    nr, _, hb = cache_ref.shape[:3]
    lane = lax.broadcasted_iota(jnp.int32, (8, LANES), 1)
    t_q = lax.broadcasted_iota(jnp.int32, (8, LANES), 0)
    bias_c = bias_c_ref[...]
    lane_c = lax.broadcasted_iota(jnp.int32, (HEAD_DIM, LANES), 1)
    nt = (((1,), (1,)), ((), ()))
    units = [(r, h) for r in range(nr) for h in range(hb)]
    slots, bias_ns = [], []
    for r in range(nr):
        n = pl.program_id(0) * nr + r
        slot = (n % REQUESTS_PER_NEW_BLOCK) * 8
        diff = t_q - (lane - slot)
        ok = (diff >= 0) & (diff <= t_q) & ((diff & (dil - 1)) == 0)
        slots.append(slot)
        bias_ns.append(jnp.where(ok, 0.0, NEG))
    s_cs = [jnp.dot(q_ref[r, h], cache_ref[r, 0, h], preferred_element_type=F32) + bias_c for r, h in units]
    s_ns = [jnp.dot(q_ref[r, h], new_ref[0, h], preferred_element_type=F32) + bias_ns[r] for r, h in units]
    ms = [jnp.maximum(jnp.max(s_c, axis=-1, keepdims=True), jnp.max(s_n, axis=-1, keepdims=True))
          for s_c, s_n in zip(s_cs, s_ns)]
    e_cs = [jnp.exp(s_c - m) for s_c, m in zip(s_cs, ms)]
    e_ns = [jnp.exp(s_n - m) for s_n, m in zip(s_ns, ms)]
    ls = [jnp.sum(e_c, axis=-1, keepdims=True) + jnp.sum(e_n, axis=-1, keepdims=True)
          for e_c, e_n in zip(e_cs, e_ns)]
    for i, (r, h) in enumerate(units):
        o = (lax.dot_general(e_cs[i], cache_ref[r, 1, h], nt, preferred_element_type=F32)
             + lax.dot_general(e_ns[i], new_ref[1, h], nt, preferred_element_type=F32))
        o_ref[r, h] = o / ls[i]
        lse_ref[r, h] = jnp.broadcast_to(ms[i] + jnp.log(ls[i]), (8, HEAD_DIM))
    for out_ref in cache_out:
        for r, h in units:
            for kv in range(2):
                shifted = pltpu.roll(cache_ref[r, kv, h], hist - 8, axis=1)
                if hist > LANES:
                    out_ref[r, kv, h, :, :hist - LANES] = shifted[:, :hist - LANES]
                moved = pltpu.roll(new_ref[kv, h], (LANES - 8) - slots[r], axis=1)
                out_ref[r, kv, h, :, hist - LANES:] = jnp.where(
                    lane_c >= LANES - 8, moved, shifted[:, hist - LANES:])


def _attn_sample(q, cache, new_t, bias_c, branch, *, dil, write_cache):
    nreq = q.shape[1]
    hist = cache.shape[-1]
    hb = min(N_HEADS, SAMPLE_CACHE_BLOCK_POSITIONS // hist)
    nr = min(REQUESTS_PER_NEW_BLOCK, max(1, SAMPLE_CACHE_BLOCK_POSITIONS // (hb * hist)))
    o_spec = pl.BlockSpec((nr, hb, 8, HEAD_DIM), lambda n, h: (n, h, 0, 0))
    out_specs = [o_spec, o_spec]
    out_shape = [jax.ShapeDtypeStruct((nreq, N_HEADS, 8, HEAD_DIM), F32)] * 2
    if write_cache:
        out_specs.append(pl.BlockSpec((nr, 2, hb, HEAD_DIM, hist), lambda n, h: (n, 0, h, 0, 0)))
        out_shape.append(jax.ShapeDtypeStruct(cache.shape, F32))
    return pl.pallas_call(
        functools.partial(_attn_sample_kernel, hist=hist, dil=dil),
        grid=(nreq // nr, N_HEADS // hb),
        in_specs=[
            pl.BlockSpec((None, nr, hb, 8, HEAD_DIM), lambda n, h: (branch, n, h, 0, 0)),
            pl.BlockSpec((nr, 2, hb, HEAD_DIM, hist), lambda n, h: (n, 0, h, 0, 0)),
            pl.BlockSpec((2, None, hb, HEAD_DIM, LANES),
                         lambda n, h: (0, branch, h, 0, n * nr // REQUESTS_PER_NEW_BLOCK)),
            pl.BlockSpec((8, hist), lambda n, h: (0, 0)),
        ],
        out_specs=out_specs,
        out_shape=out_shape,
        compiler_params=_cparams(2),
        name="attn_sample",
    )(q, cache, new_t, bias_c)


def _merge_sample_kernel(x_ref, o0, o1, o2, l0, l1, l2, w_ref, out_ref):
    ls = [l0[...], l1[...], l2[...]]
    m = jnp.maximum(jnp.maximum(ls[0], ls[1]), ls[2])
    es = [jnp.exp(l - m) for l in ls]
    den = es[0] + es[1] + es[2]
    num = es[0] * o0[...] + es[1] * o1[...] + es[2] * o2[...]
    merged = (num / den).astype(BF16)
    out_ref[...] = x_ref[...] + jnp.dot(merged, w_ref[...], preferred_element_type=F32)


def _merge_sample(x, outs, lses, w_o, layer):
    n, d = x.shape
    tile = pl.BlockSpec((n, d), lambda i: (0, 0))
    return pl.pallas_call(
        _merge_sample_kernel,
        grid=(1,),
        in_specs=[tile] * 7 + [pl.BlockSpec((None, d, d), lambda i: (layer, 0, 0))],
        out_specs=tile,
        out_shape=jax.ShapeDtypeStruct((n, d), F32),
        compiler_params=_cparams(1),
        name="merge_sample",
    )(x, *outs, *lses, w_o)


WEIGHT_PART_LANES = N_HEADS


def _merge_prompt_kernel(x_ref, o0_ref, o1_ref, o2_ref, l0_ref, l1_ref, l2_ref, e_ref, w_ref, out_ref,
                         oscr1, oscr2, lscr1, lscr2, *, tm):
    o_refs, l_refs = (o0_ref, o1_ref, o2_ref), (l0_ref, l1_ref, l2_ref)
    o_scr, l_scr = (None, oscr1, oscr2), (None, lscr1, lscr2)
    os_, ls_ = [], []
    for g, dil in enumerate(DILATIONS):
        if dil == 1:
            os_.append(o_refs[g][0, 0].astype(F32))
            ls_.append(l_refs[g][0, 0])
        else:
            for r in range(dil):
                rows = pl.ds(r, tm // dil, stride=dil)
                o_r = o_refs[g][0, r].astype(F32)
                for c in range(ATTN_WIDTH // LANES):
                    o_scr[g][c, rows, :] = o_r[:, c * LANES:(c + 1) * LANES]
                l_scr[g][rows, :] = l_refs[g][0, r]
            os_.append(jnp.concatenate([o_scr[g][c] for c in range(ATTN_WIDTH // LANES)], axis=1))
            ls_.append(l_scr[g][...])
    m = jnp.maximum(jnp.maximum(ls_[0], ls_[1]), ls_[2])
    es = [jnp.exp(l - m) for l in ls_]
    den = es[0] + es[1] + es[2]
    lane = lax.broadcasted_iota(jnp.int32, (1, LANES), 1)
    head_lane = functools.reduce(jnp.logical_or, [lane == _lse_lane(h) for h in range(N_HEADS)])
    packed = None
    for g in range(N_BRANCH):
        w = jnp.where(head_lane, es[g] / den, 0.0)
        hi = w.astype(BF16).astype(F32)
        lo = (w - hi).astype(BF16).astype(F32)
        for part, val in ((2 * g, hi), (2 * g + 1, lo)):
            if part:
                val = pltpu.roll(val, part * WEIGHT_PART_LANES, axis=1)
            packed = val if packed is None else packed + val
    spread = jnp.dot(packed.astype(BF16), e_ref[...], preferred_element_type=F32)
    merged = sum(spread[:, g * ATTN_WIDTH:(g + 1) * ATTN_WIDTH] * os_[g] for g in range(N_BRANCH))
    out_ref[...] = x_ref[...] + jnp.dot(merged.astype(BF16), w_ref[...], preferred_element_type=F32)


def _merge_prompt(x, outs, lses, spread_mat, w_o, layer, *, bsz, seq, tm):
    n, d = x.shape
    nt = seq // tm
    tile = pl.BlockSpec((tm, d), lambda b, j: (b * nt + j, 0))
    cls = lambda width: [pl.BlockSpec((1, dil, tm // dil, width), lambda b, j: (b, 0, j, 0))
                         for dil in DILATIONS]
    return pl.pallas_call(
        functools.partial(_merge_prompt_kernel, tm=tm),
        grid=(bsz, nt),
        in_specs=[tile, *cls(ATTN_WIDTH), *cls(LANES),
                  pl.BlockSpec((LANES, N_BRANCH * ATTN_WIDTH), lambda b, j: (0, 0)),
                  pl.BlockSpec((None, d, d), lambda b, j: (layer, 0, 0))],
        out_specs=tile,
        out_shape=jax.ShapeDtypeStruct((n, d), F32),
        scratch_shapes=[pltpu.VMEM((ATTN_WIDTH // LANES, tm, LANES), F32),
                        pltpu.VMEM((ATTN_WIDTH // LANES, tm, LANES), F32),
                        pltpu.VMEM((tm, LANES), F32), pltpu.VMEM((tm, LANES), F32)],
        compiler_params=_cparams(2),
        name="merge_prompt",
    )(x, *outs, *lses, spread_mat, w_o)


def _rope_tables(pos):
    inv = jnp.power(ROPE_THETA, -jnp.arange(0, HEAD_DIM, 2, dtype=F32) / HEAD_DIM)
    ang = pos.astype(F32)[:, None] * inv[None, :]
    ang = jnp.concatenate([ang, ang], axis=-1)
    cos, sin = jnp.cos(ang), jnp.sin(ang)
    first = jnp.arange(HEAD_DIM) < HEAD_DIM // 2
    sa = jnp.where(first, -sin, 0.0)
    sb = jnp.where(first, 0.0, sin)
    tile2 = lambda a: jnp.concatenate([a, a], axis=-1)
    return tile2(cos), tile2(sa), tile2(sb)


def _head_mean_matrix():
    head = np.arange(MXU_N) // HEAD_DIM
    return jnp.asarray((head[:, None] == head[None, :]) / HEAD_DIM, dtype=BF16)


def _prompt_bias():
    qi = np.arange(BLK)[:, None]
    ki = np.arange(2 * BLK)[None, :]
    rel = qi + BLK - ki
    band = (rel >= 0) & (rel <= BLK)
    first = band & (ki >= BLK)
    both = np.stack([np.tile(first, (2, 1)), np.tile(band, (2, 1))])
    return jnp.asarray(np.where(both, 0.0, NEG), dtype=F32)


def _sample_bias(hist, dil):
    t = np.arange(8)[:, None]
    l = np.arange(hist)[None, :]
    ok_c = (l >= t) & ((l - t) % dil == 0)
    return jnp.asarray(np.where(ok_c, 0.0, NEG), dtype=F32)


def _weight_spread_matrix():
    hit = np.zeros((LANES, N_BRANCH * ATTN_WIDTH), np.float32)
    used = set()
    for part in range(2 * N_BRANCH):
        for h in range(N_HEADS):
            lane = (_lse_lane(h) + part * WEIGHT_PART_LANES) % LANES
            assert lane not in used
            used.add(lane)
            col = (part // 2) * ATTN_WIDTH + h * HEAD_DIM
            hit[lane, col:col + HEAD_DIM] = 1.0
    return jnp.asarray(hit, dtype=BF16)


def kernel(x_prompt, x_sample, state_pool, cache_kv_w128, cache_kv_w512, cache_kv_w2048,
           a_norm, pool_w, pool_scale, kv_norm, w_kv, k_norm, b_norm, w_q, q_norm, w_o,
           ffn_norm, w_gate, w_up, w_down):
    bsz, seq, d = x_prompt.shape
    nreq, t_new, _ = x_sample.shape
    caches = (cache_kv_w128, cache_kv_w512, cache_kv_w2048)
    n_p, n_s = bsz * seq, nreq * t_new

    pool_w_b = pool_w.astype(BF16)
    w_kv_b, w_q_b, w_o_b = w_kv.astype(BF16), w_q.astype(BF16), w_o.astype(BF16)
    wg_b, wu_b, wd_b = w_gate, w_up, w_down

    tab_p = _rope_tables(jnp.arange(seq))
    tab_s = tuple(jnp.tile(a, (nreq, 1)) for a in _rope_tables(PAST_LEN + jnp.arange(t_new)))
    msum = _head_mean_matrix()
    k_gain = jnp.tile(k_norm[:, None, :], (1, N_HEADS, 1)).reshape(1, PROJ_COLS)
    bias_p = _prompt_bias()

    xp, xs = x_prompt, x_sample
    pool_p, pool_s = [], []
    for i in range(N_A_LAYERS):
        layer_w = (a_norm[i:i + 1], pool_w_b[i], pool_scale[i:i + 1], ffn_norm, wg_b, wu_b, wd_b, i)
        zeros_hist = jnp.zeros((bsz, HALO, d), F32)
        xp, hp = _pool_ffn_layer(xp, zeros_hist, *layer_w, nb=1, tm=TM_PROMPT, full_hist=False)
        hist_s = jnp.pad(state_pool[:, i], ((0, 0), (1, 0), (0, 0)))
        xs, hs = _pool_ffn_layer(xs, hist_s, *layer_w, nb=nreq, tm=t_new, full_hist=True)
        pool_p.append(hp[:, 1:])
        pool_s.append(hs[:, 1:])

    xp, xs = xp.reshape(n_p, d), xs.reshape(n_s, d)
    kv_norm2 = kv_norm.reshape(1, d)
    prompt = dict(bsz=bsz, seq=seq, tm=TM_PROMPT)
    k_out = _proj_prompt(xp, kv_norm2, w_kv_b, 0, k_gain, tab_p, msum, rope=True,
                         tail_kv=0, tail_arrays=None, **prompt)
    v_out = _proj_prompt(xp, kv_norm2, w_kv_b, 1, k_gain, tab_p, msum, rope=False,
                         tail_kv=1, tail_arrays=k_out[N_BRANCH:], **prompt)
    k_cls, v_cls = k_out[:N_BRANCH], v_out[:N_BRANCH]
    kv_new_p = [t.reshape(bsz, 2, N_HEADS, HEAD_DIM, t.shape[-1]).transpose(0, 4, 1, 2, 3)
                for t in v_out[N_BRANCH:]]
    k_cls = [a.reshape(bsz * dil, seq // dil, ATTN_WIDTH) for a, dil in zip(k_cls, DILATIONS)]
    v_cls = [a.reshape(bsz * dil, seq // dil, ATTN_WIDTH) for a, dil in zip(v_cls, DILATIONS)]

    kv_s = _proj_sample(xs, kv_norm2, w_kv_b, 0, k_gain, tab_s, msum, n_rope=PROJ_COLS)
    caches_t = [c.transpose(0, 2, 3, 4, 1) for c in caches]
    new_t = kv_s.T.reshape(2, N_BRANCH, N_HEADS, HEAD_DIM, n_s)
    sample_bias = [_sample_bias(c.shape[1], dil) for c, dil in zip(caches, DILATIONS)]
    spread_mat = _weight_spread_matrix()

    kv_sample_out = [None] * N_BRANCH
    for j in range(N_B_LAYERS):
        layer = N_A_LAYERS + j
        q_gain = jnp.tile(q_norm[j][:, None, :], (1, N_HEADS, 1)).reshape(1, PROJ_COLS)
        q_gain = q_gain * (HEAD_DIM ** -0.5)
        bn = b_norm[j:j + 1]
        q_cls = _proj_prompt(xp, bn, w_q_b, j, q_gain * LOG2E, tab_p, msum, rope=True,
                             tail_kv=None, tail_arrays=None, **prompt)
        q_s = _proj_sample(xs, bn, w_q_b, j, q_gain, tab_s, msum, n_rope=PROJ_COLS)
        q_s5 = q_s.reshape(nreq, t_new, N_BRANCH, N_HEADS, HEAD_DIM).transpose(2, 0, 3, 1, 4)

        outs_p, lses_p, outs_s, lses_s = [], [], [], []
        for g, dil in enumerate(DILATIONS):
            q_c = q_cls[g].reshape(bsz * dil, seq // dil, ATTN_WIDTH)
            o_c, lse_c = _attn(q_c, k_cls[g], v_cls[g], bias_p)
            outs_p.append(o_c.reshape(bsz, dil, seq // dil, ATTN_WIDTH))
            lses_p.append(lse_c.reshape(bsz, dil, seq // dil, LANES))

            res = _attn_sample(q_s5, caches_t[g], new_t, sample_bias[g], g, dil=dil, write_cache=(j == 0))
            outs_s.append(res[0].transpose(0, 2, 1, 3).reshape(n_s, ATTN_WIDTH))
            lses_s.append(res[1].transpose(0, 2, 1, 3).reshape(n_s, ATTN_WIDTH))
            if j == 0:
                kv_sample_out[g] = res[2].transpose(0, 4, 1, 2, 3)

        xp = _merge_prompt(xp, outs_p, lses_p, spread_mat, w_o_b, j, **prompt)
        xs = _merge_sample(xs, outs_s, lses_s, w_o_b, j)
        xp = _ffn(xp, ffn_norm, wg_b, wu_b, wd_b, layer, tm=TM_PROMPT)
        xs = _ffn(xs, ffn_norm, wg_b, wu_b, wd_b, layer, tm=n_s)

    return (xp.reshape(bsz, seq, d), xs.reshape(nreq, t_new, d),
            jnp.stack(pool_p, axis=1), jnp.stack(pool_s, axis=1),
            kv_new_p[0], kv_sample_out[0], kv_new_p[1], kv_sample_out[1],
            kv_new_p[2], kv_sample_out[2])
```

```python
import functools

import jax
import jax.numpy as jnp
import numpy as np
from jax import lax
from jax.experimental import pallas as pl
from jax.experimental.pallas import tpu as pltpu

F32 = jnp.float32
BF16 = jnp.bfloat16

D_MODEL = 1024
N_A_LAYERS = 2
N_B_LAYERS = 2
POOL_WINDOWS = (2, 4, 8, 16)
POOL_GROUP = 256
HALO = 16
POOL_HIST = 15
WINDOWS = (128, 512, 2048)
DILATIONS = (1, 4, 16)
N_BRANCH = 3
HEAD_DIM = 64
N_HEADS = 16
ATTN_WIDTH = N_HEADS * HEAD_DIM
D_FF = 2816
ROPE_THETA = 10000.0
EPS = 1e-6
PAST_LEN = 8192
BLK = 128
NEG = -1e30

LANES = 128
MXU_N = 256
VMEM_LIMIT = 56 * 1024 * 1024
TM_PROMPT = 512


def _cparams(n_axes):
    return pltpu.CompilerParams(dimension_semantics=("arbitrary",) * n_axes,
                                vmem_limit_bytes=VMEM_LIMIT)


def _rms(x, g):
    return x * lax.rsqrt(jnp.mean(x * x, axis=-1, keepdims=True) + EPS) * g


FF_CHUNK = MXU_N


def _swiglu_residual(x, g_ref, wg_ref, wu_ref, wd_ref):
    h = _rms(x, g_ref[...]).astype(BF16)
    acc = x
    for c in range(D_FF // FF_CHUNK):
        sl = slice(c * FF_CHUNK, (c + 1) * FF_CHUNK)
        a = jnp.dot(h, wg_ref[:, sl].astype(BF16), preferred_element_type=F32)
        b = jnp.dot(h, wu_ref[:, sl].astype(BF16), preferred_element_type=F32)
        t = (a * jax.nn.sigmoid(a) * b).astype(BF16)
        acc = acc + jnp.dot(t, wd_ref[sl, :].astype(BF16), preferred_element_type=F32)
    return acc


def _ffn_weight_specs(layer):
    idx = lambda *_: (layer, 0, 0)
    resident = dict(pipeline_mode=pl.Buffered(1))
    return [pl.BlockSpec((1, D_MODEL), lambda *_: (0, 0)),
            pl.BlockSpec((None, D_MODEL, D_FF), idx, **resident),
            pl.BlockSpec((None, D_MODEL, D_FF), idx, **resident),
            pl.BlockSpec((None, D_FF, D_MODEL), idx, **resident)]


def _pool_ffn_tile(x_ref, h, j, weights, o_ref, ho_ref, *, full_hist):
    g_ref, w_ref, sc_ref, fg_ref, wg_ref, wu_ref, wd_ref = weights
    nb, tm = x_ref.shape[:2]
    x = x_ref[...]
    u = _rms(x, g_ref[...])
    ext = jnp.concatenate([h, u], axis=1)
    ho_ref[...] = ext[:, tm:tm + HALO]
    s = ext
    sums = {}
    for step in (1, 2, 4, 8):
        s = s + pltpu.roll(s, step, axis=1)
        sums[2 * step] = s
    row = j * tm + lax.broadcasted_iota(jnp.int32, (1, tm, 1), 1)
    mixed = []
    for gi, win in enumerate(POOL_WINDOWS):
        lo, hi = gi * POOL_GROUP, (gi + 1) * POOL_GROUP
        sw = sums[win][:, HALO:, lo:hi]
        if full_hist:
            cnt = jnp.float32(win)
        else:
            cnt = jnp.minimum(row + 1, win).astype(F32)
        d = (sw / cnt - u[:, :, lo:hi]).reshape(nb * tm, POOL_GROUP)
        y = jnp.dot(d.astype(BF16), w_ref[gi], preferred_element_type=F32)
        mixed.append(x[:, :, lo:hi].reshape(nb * tm, POOL_GROUP) + y * sc_ref[:, lo:hi])
    x1 = jnp.concatenate(mixed, axis=1)
    o_ref[...] = _swiglu_residual(x1, fg_ref, wg_ref, wu_ref, wd_ref).reshape(nb, tm, D_MODEL)


def _pool_ffn_kernel(xp_ref, xph_ref, xs_ref, hs_ref, *refs, prompt_tiles, tiles_per_seq):
    weights, (op_ref, hop_ref, os_ref, hos_ref) = refs[:7], refs[7:]
    i = pl.program_id(0)

    @pl.when(i < prompt_tiles)
    def _():
        j = i % tiles_per_seq
        hist = _rms(xph_ref[...], weights[0][...])
        hist = jnp.where(j == 0, jnp.zeros_like(hist), hist)
        _pool_ffn_tile(xp_ref, hist, j, weights, op_ref, hop_ref, full_hist=False)

    @pl.when(i == prompt_tiles)
    def _():
        _pool_ffn_tile(xs_ref, hs_ref[...], 0, weights, os_ref, hos_ref, full_hist=True)


def _pool_ffn_layer(xp, xs, hist_s, a_norm, w_pool, scale, ffn_gain, wg, wu, wd, layer, *, tm):
    b, s, d = xp.shape
    n, t, _ = xs.shape
    nt = s // tm
    steps = b * nt
    r = tm // HALO
    tile = lambda i: jnp.minimum(i, steps - 1)
    const2 = lambda *shape: pl.BlockSpec(shape, lambda i: (0,) * len(shape))
    return pl.pallas_call(
        functools.partial(_pool_ffn_kernel, prompt_tiles=steps, tiles_per_seq=nt),
        grid=(steps + 1,),
        in_specs=[
            pl.BlockSpec((1, tm, d), lambda i: (tile(i) // nt, tile(i) % nt, 0)),
            pl.BlockSpec((1, HALO, d), lambda i: (tile(i) // nt, jnp.maximum(tile(i) % nt * r - 1, 0), 0)),
            const2(n, t, d), const2(n, HALO, d),
            const2(1, d), const2(4, POOL_GROUP, POOL_GROUP), const2(1, d),
            *_ffn_weight_specs(layer),
        ],
        out_specs=[
            pl.BlockSpec((1, tm, d), lambda i: (tile(i) // nt, tile(i) % nt, 0)),
            pl.BlockSpec((1, HALO, d), lambda i: (tile(i) // nt, 0, 0)),
            const2(n, t, d), const2(n, HALO, d),
        ],
        out_shape=[jax.ShapeDtypeStruct((b, s, d), F32), jax.ShapeDtypeStruct((b, HALO, d), F32),
                   jax.ShapeDtypeStruct((n, t, d), F32), jax.ShapeDtypeStruct((n, HALO, d), F32)],
        compiler_params=_cparams(1),
        name="pool_ffn_layer",
    )(xp, xp, xs, hist_s, a_norm, w_pool, scale, ffn_gain[layer:layer + 1], wg, wu, wd)


def _ffn_kernel(xp_ref, xs_ref, g_ref, wg_ref, wu_ref, wd_ref, op_ref, os_ref, *, prompt_tiles):
    i = pl.program_id(0)

    @pl.when(i < prompt_tiles)
    def _():
        op_ref[...] = _swiglu_residual(xp_ref[...], g_ref, wg_ref, wu_ref, wd_ref)

    @pl.when(i == prompt_tiles)
    def _():
        os_ref[...] = _swiglu_residual(xs_ref[...], g_ref, wg_ref, wu_ref, wd_ref)


def _ffn(xp, xs, g, wg, wu, wd, layer, *, tm):
    (n_p, d), n_s = xp.shape, xs.shape[0]
    nt = n_p // tm
    p_tile = pl.BlockSpec((tm, d), lambda i: (jnp.minimum(i, nt - 1), 0))
    s_tile = pl.BlockSpec((n_s, d), lambda i: (0, 0))
    return pl.pallas_call(
        functools.partial(_ffn_kernel, prompt_tiles=nt),
        grid=(nt + 1,),
        in_specs=[p_tile, s_tile, *_ffn_weight_specs(layer)],
        out_specs=[p_tile, s_tile],
        out_shape=[jax.ShapeDtypeStruct((n_p, d), F32), jax.ShapeDtypeStruct((n_s, d), F32)],
        compiler_params=_cparams(1),
        name="ffn",
    )(xp, xs, g[layer:layer + 1], wg, wu, wd)


PROJ_COLS = N_BRANCH * ATTN_WIDTH


def _proj_dot(h, w_ref, c):
    return jnp.dot(h, w_ref[:, c * MXU_N:(c + 1) * MXU_N], preferred_element_type=F32)


def _head_norm(y, c, hg_ref, m_ref):
    msq = jnp.dot((y * y).astype(BF16), m_ref[...], preferred_element_type=F32)
    return y * lax.rsqrt(msq + EPS) * hg_ref[:, c * MXU_N:(c + 1) * MXU_N]


def _rope_natural(y, cos, sa, sb):
    halves = []
    for p in range(MXU_N // LANES):
        z = y[:, p * LANES:(p + 1) * LANES]
        halves.append(z * cos + pltpu.roll(z, LANES - HEAD_DIM // 2, axis=1) * sa
                      + pltpu.roll(z, HEAD_DIM // 2, axis=1) * sb)
    return jnp.concatenate(halves, axis=1)


def _proj_sample_kernel(x_ref, g_ref, w_ref, hg_ref, cos_ref, sa_ref, sb_ref, m_ref, o_ref, *, n_rope):
    h = _rms(x_ref[...], g_ref[...]).astype(BF16)
    cos, sa, sb = cos_ref[...], sa_ref[...], sb_ref[...]
    for c in range(o_ref.shape[1] // MXU_N):
        y = _proj_dot(h, w_ref, c)
        if c * MXU_N < n_rope:
            y = _rope_natural(_head_norm(y, c, hg_ref, m_ref), cos, sa, sb)
        o_ref[:, c * MXU_N:(c + 1) * MXU_N] = y


def _proj_sample(x, g, w, w_index, hgain, tables, msum, *, n_rope):
    n, d = x.shape
    n_cols = w.shape[-1]
    full = lambda *shape: pl.BlockSpec(shape, lambda i: (0,) * len(shape))
    if w.ndim == 3:
        w_spec = pl.BlockSpec((None, d, n_cols), lambda i: (w_index, 0, 0))
    else:
        w_spec = full(d, n_cols)
    return pl.pallas_call(
        functools.partial(_proj_sample_kernel, n_rope=n_rope),
        grid=(1,),
        in_specs=[full(n, d), full(1, d), w_spec, full(1, n_rope),
                  full(n, LANES), full(n, LANES), full(n, LANES), full(MXU_N, MXU_N)],
        out_specs=full(n, n_cols),
        out_shape=jax.ShapeDtypeStruct((n, n_cols), F32),
        compiler_params=_cparams(1),
        name="proj_sample",
    )(x, g, w, hgain, *tables, msum)


def _proj_prompt_kernel(x_ref, g_ref, w_ref, hg_ref, cos_ref, sa_ref, sb_ref, m_ref, *refs,
                        tm, rope, tail_blocks, tail_start, tails_carried, zero_other_half):
    refs = refs[tails_carried:]
    outs, yscr = refs[:N_BRANCH], refs[-1]
    tails = refs[N_BRANCH:-1]
    j = pl.program_id(1)
    h = _rms(x_ref[...], g_ref[...]).astype(BF16)
    cos, sa, sb = cos_ref[...], sa_ref[...], sb_ref[...]
    per_branch = ATTN_WIDTH // MXU_N
    slabs = MXU_N // LANES
    n_chunks = PROJ_COLS // MXU_N
    y_next = _proj_dot(h, w_ref, 0)
    for c in range(n_chunks):
        g, cc = divmod(c, per_branch)
        y = y_next
        if c + 1 < n_chunks:
            y_next = _proj_dot(h, w_ref, c + 1)
        if rope:
            y = _rope_natural(_head_norm(y, c, hg_ref, m_ref), cos, sa, sb)
        dil = DILATIONS[g]
        if dil == 1:
            outs[g][0, 0, :, cc * MXU_N:(cc + 1) * MXU_N] = y.astype(BF16)
        if dil > 1 or tails:
            for p in range(slabs):
                yscr[c * slabs + p] = y[:, p * LANES:(p + 1) * LANES]
        if dil > 1:
            for r in range(dil):
                for p in range(slabs):
                    lo = cc * MXU_N + p * LANES
                    outs[g][0, r, :, lo:lo + LANES] = (
                        yscr[c * slabs + p, pl.ds(r, tm // dil, stride=dil), :].astype(BF16))
    for g in range(N_BRANCH if tails else 0):
        blk = tail_blocks[g]

        @pl.when(j >= tail_start[g])
        def _(g=g, blk=blk):
            for s in range(ATTN_WIDTH // LANES):
                slab = yscr[g * (ATTN_WIDTH // LANES) + s, tm - blk:, :]
                tails[g][0, 0, s * LANES:(s + 1) * LANES, :] = slab.T
            if zero_other_half:
                tails[g][0, 1] = jnp.zeros((ATTN_WIDTH, blk), F32)


def _proj_prompt(x, g, w, w_block, hgain, tables, msum, *, bsz, seq, tm, rope, tail_kv, tail_arrays):
    n, d = x.shape
    nt = seq // tm
    if w.ndim == 3:
        w_spec = pl.BlockSpec((None, d, PROJ_COLS), lambda b, j: (w_block, 0, 0), pipeline_mode=pl.Buffered(1))
    else:
        w_spec = pl.BlockSpec((d, PROJ_COLS), lambda b, j: (0, w_block), pipeline_mode=pl.Buffered(1))
    const = lambda *shape: pl.BlockSpec(shape, lambda b, j: (0,) * len(shape))
    tab = pl.BlockSpec((tm, LANES), lambda b, j: (j, 0))
    in_specs = [pl.BlockSpec((tm, d), lambda b, j: (b * nt + j, 0)), const(1, d), w_spec,
                const(1, PROJ_COLS), tab, tab, tab, const(MXU_N, MXU_N)]
    args = [x, g, w, hgain, *tables, msum]
    out_specs = [pl.BlockSpec((1, dil, tm // dil, ATTN_WIDTH), lambda b, j: (b, 0, j, 0)) for dil in DILATIONS]
    out_shape = [jax.ShapeDtypeStruct((bsz, dil, seq // dil, ATTN_WIDTH), BF16) for dil in DILATIONS]
    tail_blocks = tail_start = None
    aliases = {}
    if tail_kv is not None:
        create = tail_arrays is None
        assert create == (tail_kv == 0)
        keeps = [min(wd, seq) for wd in WINDOWS]
        tail_blocks = tuple(min(k, tm) for k in keeps)
        tail_start = tuple(nt - max(k // tm, 1) for k in keeps)
        for gi, (k, blk, j0) in enumerate(zip(keeps, tail_blocks, tail_start)):
            if create:
                spec = pl.BlockSpec((1, 2, ATTN_WIDTH, blk), lambda b, j, j0=j0: (b, 0, 0, jnp.maximum(j - j0, 0)))
            else:
                spec = pl.BlockSpec((1, 1, ATTN_WIDTH, blk),
                                    lambda b, j, j0=j0: (b, tail_kv, 0, jnp.maximum(j - j0, 0)))
                aliases[len(args)] = N_BRANCH + gi
                args.append(tail_arrays[gi])
                in_specs.append(pl.BlockSpec(memory_space=pl.ANY))
            out_specs.append(spec)
            out_shape.append(jax.ShapeDtypeStruct((bsz, 2, ATTN_WIDTH, k), F32))
    kern = functools.partial(_proj_prompt_kernel, tm=tm, rope=rope, tail_blocks=tail_blocks, tail_start=tail_start,
                             tails_carried=len(aliases), zero_other_half=tail_kv == 0)
    return pl.pallas_call(
        kern,
        grid=(bsz, nt),
        in_specs=in_specs,
        out_specs=out_specs,
        out_shape=out_shape,
        scratch_shapes=[pltpu.VMEM((PROJ_COLS // LANES, tm, LANES), F32)],
        input_output_aliases=aliases,
        compiler_params=_cparams(2),
        name="proj_prompt",
    )(*args)


ATTN_Q_BLOCKS = 4
LN2 = float(np.log(2.0))
LOG2E = float(1.0 / np.log(2.0))


def _lse_lane(head):
    return head if head % 2 == 0 else HEAD_DIM + head


def _attn_kernel(q_ref, kp_ref, kc_ref, vp_ref, vc_ref, bias_ref, o_ref, lse_ref):
    j = pl.program_id(1)
    lane = lax.broadcasted_iota(jnp.int32, (BLK, LANES), 1)
    first = lane < HEAD_DIM
    nt = (((1,), (1,)), ((), ()))
    bias_first = bias_ref[jnp.minimum(j, 1)]

    def keys_values(s, b, p, ref_prev, ref_cur):
        sl = slice(p * LANES, (p + 1) * LANES)
        prev = ref_prev[s, :, sl] if b == 0 else ref_cur[s, (b - 1) * BLK:b * BLK, sl]
        return jnp.concatenate([prev, ref_cur[s, b * BLK:(b + 1) * BLK, sl]], axis=0)

    def scores(s, b, p):
        q2 = q_ref[s, b * BLK:(b + 1) * BLK, p * LANES:(p + 1) * LANES]
        zero = jnp.zeros_like(q2)
        qq = jnp.concatenate([jnp.where(first, q2, zero), jnp.where(first, zero, q2)], axis=0)
        kk = keys_values(s, b, p, kp_ref, kc_ref)
        return lax.dot_general(qq, kk, nt, preferred_element_type=F32) + (bias_first if b == 0 else bias_ref[1])

    n_seq, n_blk = q_ref.shape[0], q_ref.shape[1] // BLK
    pairs = [(s, b, p) for s in range(n_seq) for b in range(n_blk) for p in range(N_HEADS // 2)]
    s_next = scores(*pairs[0])
    lse_all = None
    for i, (sq, b, p) in enumerate(pairs):
        s = s_next
        if i + 1 < len(pairs):
            s_next = scores(*pairs[i + 1])
        m = jnp.max(s, axis=-1, keepdims=True)
        e = jnp.exp2(s - m).astype(BF16)
        vv = keys_values(sq, b, p, vp_ref, vc_ref)
        v1 = jnp.concatenate([vv, jnp.ones_like(vv)], axis=1)
        pv = jnp.dot(e, v1, preferred_element_type=F32)
        rows, sl = slice(b * BLK, (b + 1) * BLK), slice(p * LANES, (p + 1) * LANES)
        num = jnp.where(first, pv[:BLK, :LANES], pv[BLK:, :LANES])
        den = jnp.where(first, pv[:BLK, LANES:], pv[BLK:, LANES:])
        o_ref[sq, rows, sl] = (num / den).astype(o_ref.dtype)
        lse_pair = jnp.where(first, m[:BLK], m[BLK:]) * LN2 + jnp.log(den)
        here = (lane == _lse_lane(2 * p)) | (lane == _lse_lane(2 * p + 1))
        lse_all = jnp.where(here, lse_pair, jnp.zeros_like(lse_pair) if p == 0 else lse_all)
        if p == N_HEADS // 2 - 1:
            lse_ref[sq, rows, :] = lse_all


def _attn(q, k, v, bias):
    ns, l, w = q.shape
    nq = min(ATTN_Q_BLOCKS, l // BLK)
    nsq = ATTN_Q_BLOCKS // nq
    cur = lambda s, j: (s, j, 0)
    prev = lambda s, j: (s, jnp.maximum(j * nq - 1, 0), 0)
    big, small = (nsq, nq * BLK, w), (nsq, BLK, w)
    return pl.pallas_call(
        _attn_kernel,
        grid=(ns // nsq, l // (nq * BLK)),
        in_specs=[
            pl.BlockSpec(big, cur),
            pl.BlockSpec(small, prev), pl.BlockSpec(big, cur),
            pl.BlockSpec(small, prev), pl.BlockSpec(big, cur),
            pl.BlockSpec((2, 2 * BLK, 2 * BLK), lambda s, j: (0, 0, 0)),
        ],
        out_specs=[pl.BlockSpec(big, cur), pl.BlockSpec((nsq, nq * BLK, LANES), cur)],
        out_shape=[jax.ShapeDtypeStruct((ns, l, w), BF16), jax.ShapeDtypeStruct((ns, l, LANES), F32)],
        compiler_params=_cparams(2),
        name="attn_prompt",
    )(q, k, k, v, v, bias)


SAMPLE_CACHE_BLOCK_POSITIONS = 16384


REQUESTS_PER_NEW_BLOCK = LANES // 8


def _attn_sample_kernel(q_ref, cache_ref, new_ref, bias_c_ref, o_ref, lse_ref, *cache_out, hist, dil):
    nr, _, hb = cache_ref.shape[:3]
    lane = lax.broadcasted_iota(jnp.int32, (8, LANES), 1)
    t_q = lax.broadcasted_iota(jnp.int32, (8, LANES), 0)
    bias_c = bias_c_ref[...]
    lane_c = lax.broadcasted_iota(jnp.int32, (HEAD_DIM, LANES), 1)
    nt = (((1,), (1,)), ((), ()))
    units = [(r, h) for r in range(nr) for h in range(hb)]
    slots, bias_ns = [], []
    for r in range(nr):
        n = pl.program_id(0) * nr + r
        slot = (n % REQUESTS_PER_NEW_BLOCK) * 8
        diff = t_q - (lane - slot)
        ok = (diff >= 0) & (diff <= t_q) & ((diff & (dil - 1)) == 0)
        slots.append(slot)
        bias_ns.append(jnp.where(ok, 0.0, NEG))
    s_cs = [jnp.dot(q_ref[r, h], cache_ref[r, 0, h], preferred_element_type=F32) + bias_c for r, h in units]
    s_ns = [jnp.dot(q_ref[r, h], new_ref[0, h], preferred_element_type=F32) + bias_ns[r] for r, h in units]
    ms = [jnp.maximum(jnp.max(s_c, axis=-1, keepdims=True), jnp.max(s_n, axis=-1, keepdims=True))
          for s_c, s_n in zip(s_cs, s_ns)]
    e_cs = [jnp.exp(s_c - m) for s_c, m in zip(s_cs, ms)]
    e_ns = [jnp.exp(s_n - m) for s_n, m in zip(s_ns, ms)]
    ls = [jnp.sum(e_c, axis=-1, keepdims=True) + jnp.sum(e_n, axis=-1, keepdims=True)
          for e_c, e_n in zip(e_cs, e_ns)]
    for i, (r, h) in enumerate(units):
        o = (lax.dot_general(e_cs[i], cache_ref[r, 1, h], nt, preferred_element_type=F32)
             + lax.dot_general(e_ns[i], new_ref[1, h], nt, preferred_element_type=F32))
        o_ref[r, h] = o / ls[i]
        lse_ref[r, h] = jnp.broadcast_to(ms[i] + jnp.log(ls[i]), (8, HEAD_DIM))
    for out_ref in cache_out:
        for r, h in units:
            for kv in range(2):
                shifted = pltpu.roll(cache_ref[r, kv, h], hist - 8, axis=1)
                if hist > LANES:
                    out_ref[r, kv, h, :, :hist - LANES] = shifted[:, :hist - LANES]
                moved = pltpu.roll(new_ref[kv, h], (LANES - 8) - slots[r], axis=1)
                out_ref[r, kv, h, :, hist - LANES:] = jnp.where(
                    lane_c >= LANES - 8, moved, shifted[:, hist - LANES:])


def _attn_sample(q, cache, new_t, bias_c, branch, *, dil, write_cache):
    nreq = q.shape[1]
    hist = cache.shape[-1]
    hb = min(N_HEADS, SAMPLE_CACHE_BLOCK_POSITIONS // hist)
    nr = min(REQUESTS_PER_NEW_BLOCK, max(1, SAMPLE_CACHE_BLOCK_POSITIONS // (hb * hist)))
    o_spec = pl.BlockSpec((nr, hb, 8, HEAD_DIM), lambda n, h: (n, h, 0, 0))
    out_specs = [o_spec, o_spec]
    out_shape = [jax.ShapeDtypeStruct((nreq, N_HEADS, 8, HEAD_DIM), F32)] * 2
    if write_cache:
        out_specs.append(pl.BlockSpec((nr, 2, hb, HEAD_DIM, hist), lambda n, h: (n, 0, h, 0, 0)))
        out_shape.append(jax.ShapeDtypeStruct(cache.shape, F32))
    return pl.pallas_call(
        functools.partial(_attn_sample_kernel, hist=hist, dil=dil),
        grid=(nreq // nr, N_HEADS // hb),
        in_specs=[
            pl.BlockSpec((None, nr, hb, 8, HEAD_DIM), lambda n, h: (branch, n, h, 0, 0)),
            pl.BlockSpec((nr, 2, hb, HEAD_DIM, hist), lambda n, h: (n, 0, h, 0, 0)),
            pl.BlockSpec((2, None, hb, HEAD_DIM, LANES),
                         lambda n, h: (0, branch, h, 0, n * nr // REQUESTS_PER_NEW_BLOCK)),
            pl.BlockSpec((8, hist), lambda n, h: (0, 0)),
        ],
        out_specs=out_specs,
        out_shape=out_shape,
        compiler_params=_cparams(2),
        name="attn_sample",
    )(q, cache, new_t, bias_c)


def _merge_sample_kernel(x_ref, o0, o1, o2, l0, l1, l2, w_ref, out_ref):
    ls = [l0[...], l1[...], l2[...]]
    m = jnp.maximum(jnp.maximum(ls[0], ls[1]), ls[2])
    es = [jnp.exp(l - m) for l in ls]
    den = es[0] + es[1] + es[2]
    num = es[0] * o0[...] + es[1] * o1[...] + es[2] * o2[...]
    merged = (num / den).astype(BF16)
    out_ref[...] = x_ref[...] + jnp.dot(merged, w_ref[...], preferred_element_type=F32)


def _merge_sample(x, outs, lses, w_o, layer):
    n, d = x.shape
    tile = pl.BlockSpec((n, d), lambda i: (0, 0))
    return pl.pallas_call(
        _merge_sample_kernel,
        grid=(1,),
        in_specs=[tile] * 7 + [pl.BlockSpec((None, d, d), lambda i: (layer, 0, 0))],
        out_specs=tile,
        out_shape=jax.ShapeDtypeStruct((n, d), F32),
        compiler_params=_cparams(1),
        name="merge_sample",
    )(x, *outs, *lses, w_o)


WEIGHT_PART_LANES = N_HEADS


def _merge_prompt_kernel(x_ref, o0_ref, o1_ref, o2_ref, l0_ref, l1_ref, l2_ref, e_ref, w_ref, out_ref,
                         oscr1, oscr2, lscr1, lscr2, *, tm):
    o_refs, l_refs = (o0_ref, o1_ref, o2_ref), (l0_ref, l1_ref, l2_ref)
    o_scr, l_scr = (None, oscr1, oscr2), (None, lscr1, lscr2)
    os_, ls_ = [], []
    for g, dil in enumerate(DILATIONS):
        if dil == 1:
            os_.append(o_refs[g][0, 0].astype(F32))
            ls_.append(l_refs[g][0, 0])
        else:
            for r in range(dil):
                rows = pl.ds(r, tm // dil, stride=dil)
                o_r = o_refs[g][0, r].astype(F32)
                for c in range(ATTN_WIDTH // LANES):
                    o_scr[g][c, rows, :] = o_r[:, c * LANES:(c + 1) * LANES]
                l_scr[g][rows, :] = l_refs[g][0, r]
            os_.append(jnp.concatenate([o_scr[g][c] for c in range(ATTN_WIDTH // LANES)], axis=1))
            ls_.append(l_scr[g][...])
    m = jnp.maximum(jnp.maximum(ls_[0], ls_[1]), ls_[2])
    es = [jnp.exp(l - m) for l in ls_]
    den = es[0] + es[1] + es[2]
    lane = lax.broadcasted_iota(jnp.int32, (1, LANES), 1)
    head_lane = functools.reduce(jnp.logical_or, [lane == _lse_lane(h) for h in range(N_HEADS)])
    packed = None
    for g in range(N_BRANCH):
        w = jnp.where(head_lane, es[g] / den, 0.0)
        hi = w.astype(BF16).astype(F32)
        lo = (w - hi).astype(BF16).astype(F32)
        for part, val in ((2 * g, hi), (2 * g + 1, lo)):
            if part:
                val = pltpu.roll(val, part * WEIGHT_PART_LANES, axis=1)
            packed = val if packed is None else packed + val
    spread = jnp.dot(packed.astype(BF16), e_ref[...], preferred_element_type=F32)
    merged = sum(spread[:, g * ATTN_WIDTH:(g + 1) * ATTN_WIDTH] * os_[g] for g in range(N_BRANCH))
    out_ref[...] = x_ref[...] + jnp.dot(merged.astype(BF16), w_ref[...], preferred_element_type=F32)


def _merge_prompt(x, outs, lses, spread_mat, w_o, layer, *, bsz, seq, tm):
    n, d = x.shape
    nt = seq // tm
    tile = pl.BlockSpec((tm, d), lambda b, j: (b * nt + j, 0))
    cls = lambda width: [pl.BlockSpec((1, dil, tm // dil, width), lambda b, j: (b, 0, j, 0))
                         for dil in DILATIONS]
    return pl.pallas_call(
        functools.partial(_merge_prompt_kernel, tm=tm),
        grid=(bsz, nt),
        in_specs=[tile, *cls(ATTN_WIDTH), *cls(LANES),
                  pl.BlockSpec((LANES, N_BRANCH * ATTN_WIDTH), lambda b, j: (0, 0)),
                  pl.BlockSpec((None, d, d), lambda b, j: (layer, 0, 0))],
        out_specs=tile,
        out_shape=jax.ShapeDtypeStruct((n, d), F32),
        scratch_shapes=[pltpu.VMEM((ATTN_WIDTH // LANES, tm, LANES), F32),
                        pltpu.VMEM((ATTN_WIDTH // LANES, tm, LANES), F32),
                        pltpu.VMEM((tm, LANES), F32), pltpu.VMEM((tm, LANES), F32)],
        compiler_params=_cparams(2),
        name="merge_prompt",
    )(x, *outs, *lses, spread_mat, w_o)


def _rope_tables(pos):
    inv = jnp.power(ROPE_THETA, -jnp.arange(0, HEAD_DIM, 2, dtype=F32) / HEAD_DIM)
    ang = pos.astype(F32)[:, None] * inv[None, :]
    ang = jnp.concatenate([ang, ang], axis=-1)
    cos, sin = jnp.cos(ang), jnp.sin(ang)
    first = jnp.arange(HEAD_DIM) < HEAD_DIM // 2
    sa = jnp.where(first, -sin, 0.0)
    sb = jnp.where(first, 0.0, sin)
    tile2 = lambda a: jnp.concatenate([a, a], axis=-1)
    return tile2(cos), tile2(sa), tile2(sb)


def _head_mean_matrix():
    head = np.arange(MXU_N) // HEAD_DIM
    return jnp.asarray((head[:, None] == head[None, :]) / HEAD_DIM, dtype=BF16)


def _prompt_bias():
    qi = np.arange(BLK)[:, None]
    ki = np.arange(2 * BLK)[None, :]
    rel = qi + BLK - ki
    band = (rel >= 0) & (rel <= BLK)
    first = band & (ki >= BLK)
    both = np.stack([np.tile(first, (2, 1)), np.tile(band, (2, 1))])
    return jnp.asarray(np.where(both, 0.0, NEG), dtype=F32)


def _sample_bias(hist, dil):
    t = np.arange(8)[:, None]
    l = np.arange(hist)[None, :]
    ok_c = (l >= t) & ((l - t) % dil == 0)
    return jnp.asarray(np.where(ok_c, 0.0, NEG), dtype=F32)


def _weight_spread_matrix():
    hit = np.zeros((LANES, N_BRANCH * ATTN_WIDTH), np.float32)
    used = set()
    for part in range(2 * N_BRANCH):
        for h in range(N_HEADS):
            lane = (_lse_lane(h) + part * WEIGHT_PART_LANES) % LANES
            assert lane not in used
            used.add(lane)
            col = (part // 2) * ATTN_WIDTH + h * HEAD_DIM
            hit[lane, col:col + HEAD_DIM] = 1.0
    return jnp.asarray(hit, dtype=BF16)


def kernel(x_prompt, x_sample, state_pool, cache_kv_w128, cache_kv_w512, cache_kv_w2048,
           a_norm, pool_w, pool_scale, kv_norm, w_kv, k_norm, b_norm, w_q, q_norm, w_o,
           ffn_norm, w_gate, w_up, w_down):
    bsz, seq, d = x_prompt.shape
    nreq, t_new, _ = x_sample.shape
    caches = (cache_kv_w128, cache_kv_w512, cache_kv_w2048)
    n_p, n_s = bsz * seq, nreq * t_new

    pool_w_b = pool_w.astype(BF16)
    w_kv_b, w_q_b, w_o_b = w_kv.astype(BF16), w_q.astype(BF16), w_o.astype(BF16)
    wg_b, wu_b, wd_b = w_gate, w_up, w_down

    tab_p = _rope_tables(jnp.arange(seq))
    tab_s = tuple(jnp.tile(a, (nreq, 1)) for a in _rope_tables(PAST_LEN + jnp.arange(t_new)))
    msum = _head_mean_matrix()
    k_gain = jnp.tile(k_norm[:, None, :], (1, N_HEADS, 1)).reshape(1, PROJ_COLS)
    bias_p = _prompt_bias()

    xp, xs = x_prompt, x_sample
    pool_p, pool_s = [], []
    for i in range(N_A_LAYERS):
        hist_s = jnp.pad(state_pool[:, i], ((0, 0), (1, 0), (0, 0)))
        xp, hp, xs, hs = _pool_ffn_layer(xp, xs, hist_s, a_norm[i:i + 1], pool_w_b[i], pool_scale[i:i + 1],
                                         ffn_norm, wg_b, wu_b, wd_b, i, tm=TM_PROMPT)
        pool_p.append(hp[:, 1:])
        pool_s.append(hs[:, 1:])

    xp, xs = xp.reshape(n_p, d), xs.reshape(n_s, d)
    kv_norm2 = kv_norm.reshape(1, d)
    prompt = dict(bsz=bsz, seq=seq, tm=TM_PROMPT)
    k_out = _proj_prompt(xp, kv_norm2, w_kv_b, 0, k_gain, tab_p, msum, rope=True,
                         tail_kv=0, tail_arrays=None, **prompt)
    v_out = _proj_prompt(xp, kv_norm2, w_kv_b, 1, k_gain, tab_p, msum, rope=False,
                         tail_kv=1, tail_arrays=k_out[N_BRANCH:], **prompt)
    k_cls, v_cls = k_out[:N_BRANCH], v_out[:N_BRANCH]
    kv_new_p = [t.reshape(bsz, 2, N_HEADS, HEAD_DIM, t.shape[-1]).transpose(0, 4, 1, 2, 3)
                for t in v_out[N_BRANCH:]]
    k_cls = [a.reshape(bsz * dil, seq // dil, ATTN_WIDTH) for a, dil in zip(k_cls, DILATIONS)]
    v_cls = [a.reshape(bsz * dil, seq // dil, ATTN_WIDTH) for a, dil in zip(v_cls, DILATIONS)]

    kv_s = _proj_sample(xs, kv_norm2, w_kv_b, 0, k_gain, tab_s, msum, n_rope=PROJ_COLS)
    caches_t = [c.transpose(0, 2, 3, 4, 1) for c in caches]
    new_t = kv_s.T.reshape(2, N_BRANCH, N_HEADS, HEAD_DIM, n_s)
    sample_bias = [_sample_bias(c.shape[1], dil) for c, dil in zip(caches, DILATIONS)]
    spread_mat = _weight_spread_matrix()

    kv_sample_out = [None] * N_BRANCH
    for j in range(N_B_LAYERS):
        layer = N_A_LAYERS + j
        q_gain = jnp.tile(q_norm[j][:, None, :], (1, N_HEADS, 1)).reshape(1, PROJ_COLS)
        q_gain = q_gain * (HEAD_DIM ** -0.5)
        bn = b_norm[j:j + 1]
        q_cls = _proj_prompt(xp, bn, w_q_b, j, q_gain * LOG2E, tab_p, msum, rope=True,
                             tail_kv=None, tail_arrays=None, **prompt)
        q_s = _proj_sample(xs, bn, w_q_b, j, q_gain, tab_s, msum, n_rope=PROJ_COLS)
        q_s5 = q_s.reshape(nreq, t_new, N_BRANCH, N_HEADS, HEAD_DIM).transpose(2, 0, 3, 1, 4)

        outs_p, lses_p, outs_s, lses_s = [], [], [], []
        for g, dil in enumerate(DILATIONS):
            q_c = q_cls[g].reshape(bsz * dil, seq // dil, ATTN_WIDTH)
            o_c, lse_c = _attn(q_c, k_cls[g], v_cls[g], bias_p)
            outs_p.append(o_c.reshape(bsz, dil, seq // dil, ATTN_WIDTH))
            lses_p.append(lse_c.reshape(bsz, dil, seq // dil, LANES))

            res = _attn_sample(q_s5, caches_t[g], new_t, sample_bias[g], g, dil=dil, write_cache=(j == 0))
            outs_s.append(res[0].transpose(0, 2, 1, 3).reshape(n_s, ATTN_WIDTH))
            lses_s.append(res[1].transpose(0, 2, 1, 3).reshape(n_s, ATTN_WIDTH))
            if j == 0:
                kv_sample_out[g] = res[2].transpose(0, 4, 1, 2, 3)

        xp = _merge_prompt(xp, outs_p, lses_p, spread_mat, w_o_b, j, **prompt)
        xs = _merge_sample(xs, outs_s, lses_s, w_o_b, j)
        xp, xs = _ffn(xp, xs, ffn_norm, wg_b, wu_b, wd_b, layer, tm=TM_PROMPT)

    return (xp.reshape(bsz, seq, d), xs.reshape(nreq, t_new, d),
            jnp.stack(pool_p, axis=1), jnp.stack(pool_s, axis=1),
            kv_new_p[0], kv_sample_out[0], kv_new_p[1], kv_sample_out[1],
            kv_new_p[2], kv_sample_out[2])
```

```python
import functools

import jax
import jax.numpy as jnp
import numpy as np
from jax import lax
from jax.experimental import pallas as pl
from jax.experimental.pallas import tpu as pltpu

F32 = jnp.float32
BF16 = jnp.bfloat16

D_MODEL = 1024
N_A_LAYERS = 2
N_B_LAYERS = 2
POOL_WINDOWS = (2, 4, 8, 16)
POOL_GROUP = 256
HALO = 16
POOL_HIST = 15
WINDOWS = (128, 512, 2048)
DILATIONS = (1, 4, 16)
N_BRANCH = 3
HEAD_DIM = 64
N_HEADS = 16
ATTN_WIDTH = N_HEADS * HEAD_DIM
D_FF = 2816
ROPE_THETA = 10000.0
EPS = 1e-6
PAST_LEN = 8192
BLK = 128
NEG = -1e30

LANES = 128
MXU_N = 256
VMEM_LIMIT = 56 * 1024 * 1024
TM_PROMPT = 512


def _cparams(n_axes):
    return pltpu.CompilerParams(dimension_semantics=("arbitrary",) * n_axes,
                                vmem_limit_bytes=VMEM_LIMIT)


def _rms(x, g):
    return x * lax.rsqrt(jnp.mean(x * x, axis=-1, keepdims=True) + EPS) * g


FF_CHUNK = MXU_N


def _swiglu_residual(x, g_ref, wg_ref, wu_ref, wd_ref):
    h = _rms(x, g_ref[...]).astype(BF16)
    acc = x
    for c in range(D_FF // FF_CHUNK):
        sl = slice(c * FF_CHUNK, (c + 1) * FF_CHUNK)
        a = jnp.dot(h, wg_ref[:, sl].astype(BF16), preferred_element_type=F32)
        b = jnp.dot(h, wu_ref[:, sl].astype(BF16), preferred_element_type=F32)
        t = (a * jax.nn.sigmoid(a) * b).astype(BF16)
        acc = acc + jnp.dot(t, wd_ref[sl, :].astype(BF16), preferred_element_type=F32)
    return acc


def _ffn_weight_specs(layer):
    idx = lambda *_: (layer, 0, 0)
    resident = dict(pipeline_mode=pl.Buffered(1))
    return [pl.BlockSpec((1, D_MODEL), lambda *_: (0, 0)),
            pl.BlockSpec((None, D_MODEL, D_FF), idx, **resident),
            pl.BlockSpec((None, D_MODEL, D_FF), idx, **resident),
            pl.BlockSpec((None, D_FF, D_MODEL), idx, **resident)]


def _pool_ffn_tile(x_ref, h, j, weights, o_ref, ho_ref, *, full_hist):
    g_ref, w_ref, sc_ref, fg_ref, wg_ref, wu_ref, wd_ref = weights
    nb, tm = x_ref.shape[:2]
    x = x_ref[...]
    u = _rms(x, g_ref[...])
    ext = jnp.concatenate([h, u], axis=1)
    ho_ref[...] = ext[:, tm:tm + HALO]
    s = ext
    sums = {}
    for step in (1, 2, 4, 8):
        s = s + pltpu.roll(s, step, axis=1)
        sums[2 * step] = s
    row = j * tm + lax.broadcasted_iota(jnp.int32, (1, tm, 1), 1)
    mixed = []
    for gi, win in enumerate(POOL_WINDOWS):
        lo, hi = gi * POOL_GROUP, (gi + 1) * POOL_GROUP
        sw = sums[win][:, HALO:, lo:hi]
        if full_hist:
            cnt = jnp.float32(win)
        else:
            cnt = jnp.minimum(row + 1, win).astype(F32)
        d = (sw / cnt - u[:, :, lo:hi]).reshape(nb * tm, POOL_GROUP)
        y = jnp.dot(d.astype(BF16), w_ref[gi], preferred_element_type=F32)
        mixed.append(x[:, :, lo:hi].reshape(nb * tm, POOL_GROUP) + y * sc_ref[:, lo:hi])
    x1 = jnp.concatenate(mixed, axis=1)
    o_ref[...] = _swiglu_residual(x1, fg_ref, wg_ref, wu_ref, wd_ref).reshape(nb, tm, D_MODEL)


def _pool_ffn_kernel(xp_ref, xph_ref, xs_ref, hs_ref, *refs, prompt_tiles, tiles_per_seq):
    weights, (op_ref, hop_ref, os_ref, hos_ref) = refs[:7], refs[7:]
    i = pl.program_id(0)

    @pl.when(i < prompt_tiles)
    def _():
        j = i % tiles_per_seq
        hist = _rms(xph_ref[...], weights[0][...])
        hist = jnp.where(j == 0, jnp.zeros_like(hist), hist)
        _pool_ffn_tile(xp_ref, hist, j, weights, op_ref, hop_ref, full_hist=False)

    @pl.when(i == prompt_tiles)
    def _():
        _pool_ffn_tile(xs_ref, hs_ref[...], 0, weights, os_ref, hos_ref, full_hist=True)


def _pool_ffn_layer(xp, xs, hist_s, a_norm, w_pool, scale, ffn_gain, wg, wu, wd, layer, *, tm):
    b, s, d = xp.shape
    n, t, _ = xs.shape
    nt = s // tm
    steps = b * nt
    r = tm // HALO
    tile = lambda i: jnp.minimum(i, steps - 1)
    const2 = lambda *shape: pl.BlockSpec(shape, lambda i: (0,) * len(shape))
    return pl.pallas_call(
        functools.partial(_pool_ffn_kernel, prompt_tiles=steps, tiles_per_seq=nt),
        grid=(steps + 1,),
        in_specs=[
            pl.BlockSpec((1, tm, d), lambda i: (tile(i) // nt, tile(i) % nt, 0)),
            pl.BlockSpec((1, HALO, d), lambda i: (tile(i) // nt, jnp.maximum(tile(i) % nt * r - 1, 0), 0)),
            const2(n, t, d), const2(n, HALO, d),
            const2(1, d), const2(4, POOL_GROUP, POOL_GROUP), const2(1, d),
            *_ffn_weight_specs(layer),
        ],
        out_specs=[
            pl.BlockSpec((1, tm, d), lambda i: (tile(i) // nt, tile(i) % nt, 0)),
            pl.BlockSpec((1, HALO, d), lambda i: (tile(i) // nt, 0, 0)),
            const2(n, t, d), const2(n, HALO, d),
        ],
        out_shape=[jax.ShapeDtypeStruct((b, s, d), F32), jax.ShapeDtypeStruct((b, HALO, d), F32),
                   jax.ShapeDtypeStruct((n, t, d), F32), jax.ShapeDtypeStruct((n, HALO, d), F32)],
        compiler_params=_cparams(1),
        name="pool_ffn_layer",
    )(xp, xp, xs, hist_s, a_norm, w_pool, scale, ffn_gain[layer:layer + 1], wg, wu, wd)


def _ffn_kernel(xp_ref, xs_ref, g_ref, wg_ref, wu_ref, wd_ref, op_ref, os_ref, *, prompt_tiles):
    i = pl.program_id(0)

    @pl.when(i < prompt_tiles)
    def _():
        op_ref[...] = _swiglu_residual(xp_ref[...], g_ref, wg_ref, wu_ref, wd_ref)

    @pl.when(i == prompt_tiles)
    def _():
        os_ref[...] = _swiglu_residual(xs_ref[...], g_ref, wg_ref, wu_ref, wd_ref)


def _ffn(xp, xs, g, wg, wu, wd, layer, *, tm):
    (n_p, d), n_s = xp.shape, xs.shape[0]
    nt = n_p // tm
    p_tile = pl.BlockSpec((tm, d), lambda i: (jnp.minimum(i, nt - 1), 0))
    s_tile = pl.BlockSpec((n_s, d), lambda i: (0, 0))
    return pl.pallas_call(
        functools.partial(_ffn_kernel, prompt_tiles=nt),
        grid=(nt + 1,),
        in_specs=[p_tile, s_tile, *_ffn_weight_specs(layer)],
        out_specs=[p_tile, s_tile],
        out_shape=[jax.ShapeDtypeStruct((n_p, d), F32), jax.ShapeDtypeStruct((n_s, d), F32)],
        compiler_params=_cparams(1),
        name="ffn",
    )(xp, xs, g[layer:layer + 1], wg, wu, wd)


PROJ_COLS = N_BRANCH * ATTN_WIDTH


def _proj_dot(h, w_ref, c):
    return jnp.dot(h, w_ref[:, c * MXU_N:(c + 1) * MXU_N].astype(BF16), preferred_element_type=F32)


def _head_norm(y, c, hg_ref, m_ref):
    msq = jnp.dot((y * y).astype(BF16), m_ref[...], preferred_element_type=F32)
    return y * lax.rsqrt(msq + EPS) * hg_ref[:, c * MXU_N:(c + 1) * MXU_N]


def _rope_natural(y, cos, sa, sb):
    halves = []
    for p in range(MXU_N // LANES):
        z = y[:, p * LANES:(p + 1) * LANES]
        halves.append(z * cos + pltpu.roll(z, LANES - HEAD_DIM // 2, axis=1) * sa
                      + pltpu.roll(z, HEAD_DIM // 2, axis=1) * sb)
    return jnp.concatenate(halves, axis=1)


def _proj_sample_kernel(x_ref, g_ref, w_ref, hg_ref, cos_ref, sa_ref, sb_ref, m_ref, o_ref, *, n_rope):
    h = _rms(x_ref[...], g_ref[...]).astype(BF16)
    cos, sa, sb = cos_ref[...], sa_ref[...], sb_ref[...]
    for c in range(o_ref.shape[1] // MXU_N):
        y = _proj_dot(h, w_ref, c)
        if c * MXU_N < n_rope:
            y = _rope_natural(_head_norm(y, c, hg_ref, m_ref), cos, sa, sb)
        o_ref[:, c * MXU_N:(c + 1) * MXU_N] = y


def _proj_sample(x, g, w, w_index, hgain, tables, msum, *, n_rope):
    n, d = x.shape
    n_cols = w.shape[-1]
    full = lambda *shape: pl.BlockSpec(shape, lambda i: (0,) * len(shape))
    if w.ndim == 3:
        w_spec = pl.BlockSpec((None, d, n_cols), lambda i: (w_index, 0, 0))
    else:
        w_spec = full(d, n_cols)
    return pl.pallas_call(
        functools.partial(_proj_sample_kernel, n_rope=n_rope),
        grid=(1,),
        in_specs=[full(n, d), full(1, d), w_spec, full(1, n_rope),
                  full(n, LANES), full(n, LANES), full(n, LANES), full(MXU_N, MXU_N)],
        out_specs=full(n, n_cols),
        out_shape=jax.ShapeDtypeStruct((n, n_cols), F32),
        compiler_params=_cparams(1),
        name="proj_sample",
    )(x, g, w, hgain, *tables, msum)


def _proj_prompt_kernel(x_ref, g_ref, w_ref, hg_ref, cos_ref, sa_ref, sb_ref, m_ref, *refs,
                        tm, rope, tail_blocks, tail_start, tails_carried, zero_other_half):
    refs = refs[tails_carried:]
    outs, yscr = refs[:N_BRANCH], refs[-1]
    tails = refs[N_BRANCH:-1]
    j = pl.program_id(1)
    h = _rms(x_ref[...], g_ref[...]).astype(BF16)
    cos, sa, sb = cos_ref[...], sa_ref[...], sb_ref[...]
    per_branch = ATTN_WIDTH // MXU_N
    slabs = MXU_N // LANES
    n_chunks = PROJ_COLS // MXU_N
    y_next = _proj_dot(h, w_ref, 0)
    for c in range(n_chunks):
        g, cc = divmod(c, per_branch)
        y = y_next
        if c + 1 < n_chunks:
            y_next = _proj_dot(h, w_ref, c + 1)
        if rope:
            y = _rope_natural(_head_norm(y, c, hg_ref, m_ref), cos, sa, sb)
        dil = DILATIONS[g]
        if dil == 1:
            outs[g][0, 0, :, cc * MXU_N:(cc + 1) * MXU_N] = y.astype(BF16)
        if dil > 1 or tails:
            for p in range(slabs):
                yscr[c * slabs + p] = y[:, p * LANES:(p + 1) * LANES]
        if dil > 1:
            for r in range(dil):
                for p in range(slabs):
                    lo = cc * MXU_N + p * LANES
                    outs[g][0, r, :, lo:lo + LANES] = (
                        yscr[c * slabs + p, pl.ds(r, tm // dil, stride=dil), :].astype(BF16))
    for g in range(N_BRANCH if tails else 0):
        blk = tail_blocks[g]

        @pl.when(j >= tail_start[g])
        def _(g=g, blk=blk):
            for s in range(ATTN_WIDTH // LANES):
                slab = yscr[g * (ATTN_WIDTH // LANES) + s, tm - blk:, :]
                tails[g][0, 0, s * LANES:(s + 1) * LANES, :] = slab.T
            if zero_other_half:
                tails[g][0, 1] = jnp.zeros((ATTN_WIDTH, blk), F32)


def _proj_prompt(x, g, w, w_block, hgain, tables, msum, *, bsz, seq, tm, rope, tail_kv, tail_arrays):
    n, d = x.shape
    nt = seq // tm
    if w.ndim == 3:
        w_spec = pl.BlockSpec((None, d, PROJ_COLS), lambda b, j: (w_block, 0, 0), pipeline_mode=pl.Buffered(1))
    else:
        w_spec = pl.BlockSpec((d, PROJ_COLS), lambda b, j: (0, w_block), pipeline_mode=pl.Buffered(1))
    const = lambda *shape: pl.BlockSpec(shape, lambda b, j: (0,) * len(shape))
    tab = pl.BlockSpec((tm, LANES), lambda b, j: (j, 0))
    in_specs = [pl.BlockSpec((tm, d), lambda b, j: (b * nt + j, 0)), const(1, d), w_spec,
                const(1, PROJ_COLS), tab, tab, tab, const(MXU_N, MXU_N)]
    args = [x, g, w, hgain, *tables, msum]
    out_specs = [pl.BlockSpec((1, dil, tm // dil, ATTN_WIDTH), lambda b, j: (b, 0, j, 0)) for dil in DILATIONS]
    out_shape = [jax.ShapeDtypeStruct((bsz, dil, seq // dil, ATTN_WIDTH), BF16) for dil in DILATIONS]
    tail_blocks = tail_start = None
    aliases = {}
    if tail_kv is not None:
        create = tail_arrays is None
        assert create == (tail_kv == 0)
        keeps = [min(wd, seq) for wd in WINDOWS]
        tail_blocks = tuple(min(k, tm) for k in keeps)
        tail_start = tuple(nt - max(k // tm, 1) for k in keeps)
        for gi, (k, blk, j0) in enumerate(zip(keeps, tail_blocks, tail_start)):
            if create:
                spec = pl.BlockSpec((1, 2, ATTN_WIDTH, blk), lambda b, j, j0=j0: (b, 0, 0, jnp.maximum(j - j0, 0)))
            else:
                spec = pl.BlockSpec((1, 1, ATTN_WIDTH, blk),
                                    lambda b, j, j0=j0: (b, tail_kv, 0, jnp.maximum(j - j0, 0)))
                aliases[len(args)] = N_BRANCH + gi
                args.append(tail_arrays[gi])
                in_specs.append(pl.BlockSpec(memory_space=pl.ANY))
            out_specs.append(spec)
            out_shape.append(jax.ShapeDtypeStruct((bsz, 2, ATTN_WIDTH, k), F32))
    kern = functools.partial(_proj_prompt_kernel, tm=tm, rope=rope, tail_blocks=tail_blocks, tail_start=tail_start,
                             tails_carried=len(aliases), zero_other_half=tail_kv == 0)
    return pl.pallas_call(
        kern,
        grid=(bsz, nt),
        in_specs=in_specs,
        out_specs=out_specs,
        out_shape=out_shape,
        scratch_shapes=[pltpu.VMEM((PROJ_COLS // LANES, tm, LANES), F32)],
        input_output_aliases=aliases,
        compiler_params=_cparams(2),
        name="proj_prompt",
    )(*args)


ATTN_Q_BLOCKS = 4
LN2 = float(np.log(2.0))
LOG2E = float(1.0 / np.log(2.0))


def _lse_lane(head):
    return head if head % 2 == 0 else HEAD_DIM + head


def _attn_kernel(q_ref, kp_ref, kc_ref, vp_ref, vc_ref, bias_ref, o_ref, lse_ref):
    j = pl.program_id(1)
    lane = lax.broadcasted_iota(jnp.int32, (BLK, LANES), 1)
    first = lane < HEAD_DIM
    nt = (((1,), (1,)), ((), ()))
    bias_first = bias_ref[jnp.minimum(j, 1)]

    def keys_values(s, b, p, ref_prev, ref_cur):
        sl = slice(p * LANES, (p + 1) * LANES)
        prev = ref_prev[s, :, sl] if b == 0 else ref_cur[s, (b - 1) * BLK:b * BLK, sl]
        return jnp.concatenate([prev, ref_cur[s, b * BLK:(b + 1) * BLK, sl]], axis=0)

    def scores(s, b, p):
        q2 = q_ref[s, b * BLK:(b + 1) * BLK, p * LANES:(p + 1) * LANES]
        zero = jnp.zeros_like(q2)
        qq = jnp.concatenate([jnp.where(first, q2, zero), jnp.where(first, zero, q2)], axis=0)
        kk = keys_values(s, b, p, kp_ref, kc_ref)
        return lax.dot_general(qq, kk, nt, preferred_element_type=F32) + (bias_first if b == 0 else bias_ref[1])

    n_seq, n_blk = q_ref.shape[0], q_ref.shape[1] // BLK
    pairs = [(s, b, p) for s in range(n_seq) for b in range(n_blk) for p in range(N_HEADS // 2)]
    s_next = scores(*pairs[0])
    lse_all = None
    for i, (sq, b, p) in enumerate(pairs):
        s = s_next
        if i + 1 < len(pairs):
            s_next = scores(*pairs[i + 1])
        m = jnp.max(s, axis=-1, keepdims=True)
        e = jnp.exp2(s - m).astype(BF16)
        vv = keys_values(sq, b, p, vp_ref, vc_ref)
        v1 = jnp.concatenate([vv, jnp.ones_like(vv)], axis=1)
        pv = jnp.dot(e, v1, preferred_element_type=F32)
        rows, sl = slice(b * BLK, (b + 1) * BLK), slice(p * LANES, (p + 1) * LANES)
        num = jnp.where(first, pv[:BLK, :LANES], pv[BLK:, :LANES])
        den = jnp.where(first, pv[:BLK, LANES:], pv[BLK:, LANES:])
        o_ref[sq, rows, sl] = (num / den).astype(o_ref.dtype)
        lse_pair = jnp.where(first, m[:BLK], m[BLK:]) * LN2 + jnp.log(den)
        here = (lane == _lse_lane(2 * p)) | (lane == _lse_lane(2 * p + 1))
        lse_all = jnp.where(here, lse_pair, jnp.zeros_like(lse_pair) if p == 0 else lse_all)
        if p == N_HEADS // 2 - 1:
            lse_ref[sq, rows, :] = lse_all


def _attn(q, k, v, bias):
    ns, l, w = q.shape
    nq = min(ATTN_Q_BLOCKS, l // BLK)
    nsq = ATTN_Q_BLOCKS // nq
    cur = lambda s, j: (s, j, 0)
    prev = lambda s, j: (s, jnp.maximum(j * nq - 1, 0), 0)
    big, small = (nsq, nq * BLK, w), (nsq, BLK, w)
    return pl.pallas_call(
        _attn_kernel,
        grid=(ns // nsq, l // (nq * BLK)),
        in_specs=[
            pl.BlockSpec(big, cur),
            pl.BlockSpec(small, prev), pl.BlockSpec(big, cur),
            pl.BlockSpec(small, prev), pl.BlockSpec(big, cur),
            pl.BlockSpec((2, 2 * BLK, 2 * BLK), lambda s, j: (0, 0, 0)),
        ],
        out_specs=[pl.BlockSpec(big, cur), pl.BlockSpec((nsq, nq * BLK, LANES), cur)],
        out_shape=[jax.ShapeDtypeStruct((ns, l, w), BF16), jax.ShapeDtypeStruct((ns, l, LANES), F32)],
        compiler_params=_cparams(2),
        name="attn_prompt",
    )(q, k, k, v, v, bias)


SAMPLE_CACHE_BLOCK_POSITIONS = 16384


REQUESTS_PER_NEW_BLOCK = LANES // 8


def _attn_sample_kernel(q_ref, cache_ref, new_ref, bias_c_ref, o_ref, lse_ref, *cache_out, hist, dil):
    nr, _, hb = cache_ref.shape[:3]
    lane = lax.broadcasted_iota(jnp.int32, (8, LANES), 1)
    t_q = lax.broadcasted_iota(jnp.int32, (8, LANES), 0)
    bias_c = bias_c_ref[...]
    lane_c = lax.broadcasted_iota(jnp.int32, (HEAD_DIM, LANES), 1)
    nt = (((1,), (1,)), ((), ()))
    units = [(r, h) for r in range(nr) for h in range(hb)]
    slots, bias_ns = [], []
    for r in range(nr):
        n = pl.program_id(0) * nr + r
        slot = (n % REQUESTS_PER_NEW_BLOCK) * 8
        diff = t_q - (lane - slot)
        ok = (diff >= 0) & (diff <= t_q) & ((diff & (dil - 1)) == 0)
        slots.append(slot)
        bias_ns.append(jnp.where(ok, 0.0, NEG))
    s_cs = [jnp.dot(q_ref[r, h], cache_ref[r, 0, h], preferred_element_type=F32) + bias_c for r, h in units]
    s_ns = [jnp.dot(q_ref[r, h], new_ref[0, h], preferred_element_type=F32) + bias_ns[r] for r, h in units]
    ms = [jnp.maximum(jnp.max(s_c, axis=-1, keepdims=True), jnp.max(s_n, axis=-1, keepdims=True))
          for s_c, s_n in zip(s_cs, s_ns)]
    e_cs = [jnp.exp(s_c - m) for s_c, m in zip(s_cs, ms)]
    e_ns = [jnp.exp(s_n - m) for s_n, m in zip(s_ns, ms)]
    ls = [jnp.sum(e_c, axis=-1, keepdims=True) + jnp.sum(e_n, axis=-1, keepdims=True)
          for e_c, e_n in zip(e_cs, e_ns)]
    for i, (r, h) in enumerate(units):
        o = (lax.dot_general(e_cs[i], cache_ref[r, 1, h], nt, preferred_element_type=F32)
             + lax.dot_general(e_ns[i], new_ref[1, h], nt, preferred_element_type=F32))
        o_ref[r, h] = o / ls[i]
        lse_ref[r, h] = jnp.broadcast_to(ms[i] + jnp.log(ls[i]), (8, HEAD_DIM))
    for out_ref in cache_out:
        for r, h in units:
            for kv in range(2):
                shifted = pltpu.roll(cache_ref[r, kv, h], hist - 8, axis=1)
                if hist > LANES:
                    out_ref[r, kv, h, :, :hist - LANES] = shifted[:, :hist - LANES]
                moved = pltpu.roll(new_ref[kv, h], (LANES - 8) - slots[r], axis=1)
                out_ref[r, kv, h, :, hist - LANES:] = jnp.where(
                    lane_c >= LANES - 8, moved, shifted[:, hist - LANES:])


def _attn_sample(q, cache, new_t, bias_c, branch, *, dil, write_cache):
    nreq = q.shape[1]
    hist = cache.shape[-1]
    hb = min(N_HEADS, SAMPLE_CACHE_BLOCK_POSITIONS // hist)
    nr = min(REQUESTS_PER_NEW_BLOCK, max(1, SAMPLE_CACHE_BLOCK_POSITIONS // (hb * hist)))
    o_spec = pl.BlockSpec((nr, hb, 8, HEAD_DIM), lambda n, h: (n, h, 0, 0))
    out_specs = [o_spec, o_spec]
    out_shape = [jax.ShapeDtypeStruct((nreq, N_HEADS, 8, HEAD_DIM), F32)] * 2
    if write_cache:
        out_specs.append(pl.BlockSpec((nr, 2, hb, HEAD_DIM, hist), lambda n, h: (n, 0, h, 0, 0)))
        out_shape.append(jax.ShapeDtypeStruct(cache.shape, F32))
    return pl.pallas_call(
        functools.partial(_attn_sample_kernel, hist=hist, dil=dil),
        grid=(nreq // nr, N_HEADS // hb),
        in_specs=[
            pl.BlockSpec((None, nr, hb, 8, HEAD_DIM), lambda n, h: (branch, n, h, 0, 0)),
            pl.BlockSpec((nr, 2, hb, HEAD_DIM, hist), lambda n, h: (n, 0, h, 0, 0)),
            pl.BlockSpec((2, None, hb, HEAD_DIM, LANES),
                         lambda n, h: (0, branch, h, 0, n * nr // REQUESTS_PER_NEW_BLOCK)),
            pl.BlockSpec((8, hist), lambda n, h: (0, 0)),
        ],
        out_specs=out_specs,
        out_shape=out_shape,
        compiler_params=_cparams(2),
        name="attn_sample",
    )(q, cache, new_t, bias_c)


def _merge_sample_kernel(x_ref, o0, o1, o2, l0, l1, l2, w_ref, out_ref):
    ls = [l0[...], l1[...], l2[...]]
    m = jnp.maximum(jnp.maximum(ls[0], ls[1]), ls[2])
    es = [jnp.exp(l - m) for l in ls]
    den = es[0] + es[1] + es[2]
    num = es[0] * o0[...] + es[1] * o1[...] + es[2] * o2[...]
    merged = (num / den).astype(BF16)
    out_ref[...] = x_ref[...] + jnp.dot(merged, w_ref[...].astype(BF16), preferred_element_type=F32)


def _merge_sample(x, outs, lses, w_o, layer):
    n, d = x.shape
    tile = pl.BlockSpec((n, d), lambda i: (0, 0))
    return pl.pallas_call(
        _merge_sample_kernel,
        grid=(1,),
        in_specs=[tile] * 7 + [pl.BlockSpec((None, d, d), lambda i: (layer, 0, 0))],
        out_specs=tile,
        out_shape=jax.ShapeDtypeStruct((n, d), F32),
        compiler_params=_cparams(1),
        name="merge_sample",
    )(x, *outs, *lses, w_o)


WEIGHT_PART_LANES = N_HEADS


def _merge_prompt_kernel(x_ref, o0_ref, o1_ref, o2_ref, l0_ref, l1_ref, l2_ref, e_ref, w_ref, out_ref,
                         oscr1, oscr2, lscr1, lscr2, *, tm):
    o_refs, l_refs = (o0_ref, o1_ref, o2_ref), (l0_ref, l1_ref, l2_ref)
    o_scr, l_scr = (None, oscr1, oscr2), (None, lscr1, lscr2)
    os_, ls_ = [], []
    for g, dil in enumerate(DILATIONS):
        if dil == 1:
            os_.append(o_refs[g][0, 0].astype(F32))
            ls_.append(l_refs[g][0, 0])
        else:
            for r in range(dil):
                rows = pl.ds(r, tm // dil, stride=dil)
                o_r = o_refs[g][0, r].astype(F32)
                for c in range(ATTN_WIDTH // LANES):
                    o_scr[g][c, rows, :] = o_r[:, c * LANES:(c + 1) * LANES]
                l_scr[g][rows, :] = l_refs[g][0, r]
            os_.append(jnp.concatenate([o_scr[g][c] for c in range(ATTN_WIDTH // LANES)], axis=1))
            ls_.append(l_scr[g][...])
    m = jnp.maximum(jnp.maximum(ls_[0], ls_[1]), ls_[2])
    es = [jnp.exp(l - m) for l in ls_]
    den = es[0] + es[1] + es[2]
    lane = lax.broadcasted_iota(jnp.int32, (1, LANES), 1)
    head_lane = functools.reduce(jnp.logical_or, [lane == _lse_lane(h) for h in range(N_HEADS)])
    packed = None
    for g in range(N_BRANCH):
        w = jnp.where(head_lane, es[g] / den, 0.0)
        hi = w.astype(BF16).astype(F32)
        lo = (w - hi).astype(BF16).astype(F32)
        for part, val in ((2 * g, hi), (2 * g + 1, lo)):
            if part:
                val = pltpu.roll(val, part * WEIGHT_PART_LANES, axis=1)
            packed = val if packed is None else packed + val
    spread = jnp.dot(packed.astype(BF16), e_ref[...], preferred_element_type=F32)
    merged = sum(spread[:, g * ATTN_WIDTH:(g + 1) * ATTN_WIDTH] * os_[g] for g in range(N_BRANCH))
    out_ref[...] = x_ref[...] + jnp.dot(merged.astype(BF16), w_ref[...].astype(BF16), preferred_element_type=F32)


def _merge_prompt(x, outs, lses, spread_mat, w_o, layer, *, bsz, seq, tm):
    n, d = x.shape
    nt = seq // tm
    tile = pl.BlockSpec((tm, d), lambda b, j: (b * nt + j, 0))
    cls = lambda width: [pl.BlockSpec((1, dil, tm // dil, width), lambda b, j: (b, 0, j, 0))
                         for dil in DILATIONS]
    return pl.pallas_call(
        functools.partial(_merge_prompt_kernel, tm=tm),
        grid=(bsz, nt),
        in_specs=[tile, *cls(ATTN_WIDTH), *cls(LANES),
                  pl.BlockSpec((LANES, N_BRANCH * ATTN_WIDTH), lambda b, j: (0, 0)),
                  pl.BlockSpec((None, d, d), lambda b, j: (layer, 0, 0))],
        out_specs=tile,
        out_shape=jax.ShapeDtypeStruct((n, d), F32),
        scratch_shapes=[pltpu.VMEM((ATTN_WIDTH // LANES, tm, LANES), F32),
                        pltpu.VMEM((ATTN_WIDTH // LANES, tm, LANES), F32),
                        pltpu.VMEM((tm, LANES), F32), pltpu.VMEM((tm, LANES), F32)],
        compiler_params=_cparams(2),
        name="merge_prompt",
    )(x, *outs, *lses, spread_mat, w_o)


def _rope_tables(pos):
    inv = jnp.power(ROPE_THETA, -jnp.arange(0, HEAD_DIM, 2, dtype=F32) / HEAD_DIM)
    ang = pos.astype(F32)[:, None] * inv[None, :]
    ang = jnp.concatenate([ang, ang], axis=-1)
    cos, sin = jnp.cos(ang), jnp.sin(ang)
    first = jnp.arange(HEAD_DIM) < HEAD_DIM // 2
    sa = jnp.where(first, -sin, 0.0)
    sb = jnp.where(first, 0.0, sin)
    tile2 = lambda a: jnp.concatenate([a, a], axis=-1)
    return tile2(cos), tile2(sa), tile2(sb)


def _head_mean_matrix():
    head = np.arange(MXU_N) // HEAD_DIM
    return jnp.asarray((head[:, None] == head[None, :]) / HEAD_DIM, dtype=BF16)


def _prompt_bias():
    qi = np.arange(BLK)[:, None]
    ki = np.arange(2 * BLK)[None, :]
    rel = qi + BLK - ki
    band = (rel >= 0) & (rel <= BLK)
    first = band & (ki >= BLK)
    both = np.stack([np.tile(first, (2, 1)), np.tile(band, (2, 1))])
    return jnp.asarray(np.where(both, 0.0, NEG), dtype=F32)


def _sample_bias(hist, dil):
    t = np.arange(8)[:, None]
    l = np.arange(hist)[None, :]
    ok_c = (l >= t) & ((l - t) % dil == 0)
    return jnp.asarray(np.where(ok_c, 0.0, NEG), dtype=F32)


def _weight_spread_matrix():
    hit = np.zeros((LANES, N_BRANCH * ATTN_WIDTH), np.float32)
    used = set()
    for part in range(2 * N_BRANCH):
        for h in range(N_HEADS):
            lane = (_lse_lane(h) + part * WEIGHT_PART_LANES) % LANES
            assert lane not in used
            used.add(lane)
            col = (part // 2) * ATTN_WIDTH + h * HEAD_DIM
            hit[lane, col:col + HEAD_DIM] = 1.0
    return jnp.asarray(hit, dtype=BF16)


def kernel(x_prompt, x_sample, state_pool, cache_kv_w128, cache_kv_w512, cache_kv_w2048,
           a_norm, pool_w, pool_scale, kv_norm, w_kv, k_norm, b_norm, w_q, q_norm, w_o,
           ffn_norm, w_gate, w_up, w_down):
    bsz, seq, d = x_prompt.shape
    nreq, t_new, _ = x_sample.shape
    caches = (cache_kv_w128, cache_kv_w512, cache_kv_w2048)
    n_p, n_s = bsz * seq, nreq * t_new

    pool_w_b = pool_w.astype(BF16)
    w_kv_b, w_q_b, w_o_b = w_kv, w_q, w_o
    wg_b, wu_b, wd_b = w_gate, w_up, w_down

    tab_p = _rope_tables(jnp.arange(seq))
    tab_s = tuple(jnp.tile(a, (nreq, 1)) for a in _rope_tables(PAST_LEN + jnp.arange(t_new)))
    msum = _head_mean_matrix()
    k_gain = jnp.tile(k_norm[:, None, :], (1, N_HEADS, 1)).reshape(1, PROJ_COLS)
    bias_p = _prompt_bias()

    xp, xs = x_prompt, x_sample
    pool_p, pool_s = [], []
    for i in range(N_A_LAYERS):
        hist_s = jnp.pad(state_pool[:, i], ((0, 0), (1, 0), (0, 0)))
        xp, hp, xs, hs = _pool_ffn_layer(xp, xs, hist_s, a_norm[i:i + 1], pool_w_b[i], pool_scale[i:i + 1],
                                         ffn_norm, wg_b, wu_b, wd_b, i, tm=TM_PROMPT)
        pool_p.append(hp[:, 1:])
        pool_s.append(hs[:, 1:])

    xp, xs = xp.reshape(n_p, d), xs.reshape(n_s, d)
    kv_norm2 = kv_norm.reshape(1, d)
    prompt = dict(bsz=bsz, seq=seq, tm=TM_PROMPT)
    k_out = _proj_prompt(xp, kv_norm2, w_kv_b, 0, k_gain, tab_p, msum, rope=True,
                         tail_kv=0, tail_arrays=None, **prompt)
    v_out = _proj_prompt(xp, kv_norm2, w_kv_b, 1, k_gain, tab_p, msum, rope=False,
                         tail_kv=1, tail_arrays=k_out[N_BRANCH:], **prompt)
    k_cls, v_cls = k_out[:N_BRANCH], v_out[:N_BRANCH]
    kv_new_p = [t.reshape(bsz, 2, N_HEADS, HEAD_DIM, t.shape[-1]).transpose(0, 4, 1, 2, 3)
                for t in v_out[N_BRANCH:]]
    k_cls = [a.reshape(bsz * dil, seq // dil, ATTN_WIDTH) for a, dil in zip(k_cls, DILATIONS)]
    v_cls = [a.reshape(bsz * dil, seq // dil, ATTN_WIDTH) for a, dil in zip(v_cls, DILATIONS)]

    kv_s = _proj_sample(xs, kv_norm2, w_kv_b, 0, k_gain, tab_s, msum, n_rope=PROJ_COLS)
    caches_t = [c.transpose(0, 2, 3, 4, 1) for c in caches]
    new_t = kv_s.T.reshape(2, N_BRANCH, N_HEADS, HEAD_DIM, n_s)
    sample_bias = [_sample_bias(c.shape[1], dil) for c, dil in zip(caches, DILATIONS)]
    spread_mat = _weight_spread_matrix()

    kv_sample_out = [None] * N_BRANCH
    for j in range(N_B_LAYERS):
        layer = N_A_LAYERS + j
        q_gain = jnp.tile(q_norm[j][:, None, :], (1, N_HEADS, 1)).reshape(1, PROJ_COLS)
        q_gain = q_gain * (HEAD_DIM ** -0.5)
        bn = b_norm[j:j + 1]
        q_cls = _proj_prompt(xp, bn, w_q_b, j, q_gain * LOG2E, tab_p, msum, rope=True,
                             tail_kv=None, tail_arrays=None, **prompt)
        q_s = _proj_sample(xs, bn, w_q_b, j, q_gain, tab_s, msum, n_rope=PROJ_COLS)
        q_s5 = q_s.reshape(nreq, t_new, N_BRANCH, N_HEADS, HEAD_DIM).transpose(2, 0, 3, 1, 4)

        outs_p, lses_p, outs_s, lses_s = [], [], [], []
        for g, dil in enumerate(DILATIONS):
            q_c = q_cls[g].reshape(bsz * dil, seq // dil, ATTN_WIDTH)
            o_c, lse_c = _attn(q_c, k_cls[g], v_cls[g], bias_p)
            outs_p.append(o_c.reshape(bsz, dil, seq // dil, ATTN_WIDTH))
            lses_p.append(lse_c.reshape(bsz, dil, seq // dil, LANES))

            res = _attn_sample(q_s5, caches_t[g], new_t, sample_bias[g], g, dil=dil, write_cache=(j == 0))
            outs_s.append(res[0].transpose(0, 2, 1, 3).reshape(n_s, ATTN_WIDTH))
            lses_s.append(res[1].transpose(0, 2, 1, 3).reshape(n_s, ATTN_WIDTH))
            if j == 0:
                kv_sample_out[g] = res[2].transpose(0, 4, 1, 2, 3)

        xp = _merge_prompt(xp, outs_p, lses_p, spread_mat, w_o_b, j, **prompt)
        xs = _merge_sample(xs, outs_s, lses_s, w_o_b, j)
        xp, xs = _ffn(xp, xs, ffn_norm, wg_b, wu_b, wd_b, layer, tm=TM_PROMPT)

    return (xp.reshape(bsz, seq, d), xs.reshape(nreq, t_new, d),
            jnp.stack(pool_p, axis=1), jnp.stack(pool_s, axis=1),
            kv_new_p[0], kv_sample_out[0], kv_new_p[1], kv_sample_out[1],
            kv_new_p[2], kv_sample_out[2])
```

```python
import functools

import jax
import jax.numpy as jnp
import numpy as np
from jax import lax
from jax.experimental import pallas as pl
from jax.experimental.pallas import tpu as pltpu

F32 = jnp.float32
BF16 = jnp.bfloat16

D_MODEL = 1024
N_A_LAYERS = 2
N_B_LAYERS = 2
POOL_WINDOWS = (2, 4, 8, 16)
POOL_GROUP = 256
HALO = 16
POOL_HIST = 15
WINDOWS = (128, 512, 2048)
DILATIONS = (1, 4, 16)
N_BRANCH = 3
HEAD_DIM = 64
N_HEADS = 16
ATTN_WIDTH = N_HEADS * HEAD_DIM
D_FF = 2816
ROPE_THETA = 10000.0
EPS = 1e-6
PAST_LEN = 8192
BLK = 128
NEG = -1e30

LANES = 128
MXU_N = 256
VMEM_LIMIT = 56 * 1024 * 1024
TM_PROMPT = 512


def _cparams(n_axes):
    return pltpu.CompilerParams(dimension_semantics=("arbitrary",) * n_axes,
                                vmem_limit_bytes=VMEM_LIMIT)


def _rms(x, g):
    return x * lax.rsqrt(jnp.mean(x * x, axis=-1, keepdims=True) + EPS) * g


FF_CHUNK = MXU_N


def _swiglu_residual(x, g_ref, wg_ref, wu_ref, wd_ref):
    h = _rms(x, g_ref[...]).astype(BF16)
    acc = x
    for c in range(D_FF // FF_CHUNK):
        sl = slice(c * FF_CHUNK, (c + 1) * FF_CHUNK)
        a = jnp.dot(h, wg_ref[:, sl].astype(BF16), preferred_element_type=F32)
        b = jnp.dot(h, wu_ref[:, sl].astype(BF16), preferred_element_type=F32)
        t = (a * jax.nn.sigmoid(a) * b).astype(BF16)
        acc = acc + jnp.dot(t, wd_ref[sl, :].astype(BF16), preferred_element_type=F32)
    return acc


def _ffn_weight_specs(layer):
    idx = lambda *_: (layer, 0, 0)
    resident = dict(pipeline_mode=pl.Buffered(1))
    return [pl.BlockSpec((1, D_MODEL), lambda *_: (0, 0)),
            pl.BlockSpec((None, D_MODEL, D_FF), idx, **resident),
            pl.BlockSpec((None, D_MODEL, D_FF), idx, **resident),
            pl.BlockSpec((None, D_FF, D_MODEL), idx, **resident)]


def _pool_ffn_tile(x_ref, h, j, weights, o_ref, ho_ref, *, full_hist):
    g_ref, w_ref, sc_ref, fg_ref, wg_ref, wu_ref, wd_ref = weights
    nb, tm = x_ref.shape[:2]
    x = x_ref[...]
    u = _rms(x, g_ref[...])
    ext = jnp.concatenate([h, u], axis=1)
    ho_ref[...] = ext[:, tm:tm + HALO]
    s = ext
    sums = {}
    for step in (1, 2, 4, 8):
        s = s + pltpu.roll(s, step, axis=1)
        sums[2 * step] = s
    row = j * tm + lax.broadcasted_iota(jnp.int32, (1, tm, 1), 1)
    mixed = []
    for gi, win in enumerate(POOL_WINDOWS):
        lo, hi = gi * POOL_GROUP, (gi + 1) * POOL_GROUP
        sw = sums[win][:, HALO:, lo:hi]
        if full_hist:
            cnt = jnp.float32(win)
        else:
            cnt = jnp.minimum(row + 1, win).astype(F32)
        d = (sw / cnt - u[:, :, lo:hi]).reshape(nb * tm, POOL_GROUP)
        y = jnp.dot(d.astype(BF16), w_ref[gi], preferred_element_type=F32)
        mixed.append(x[:, :, lo:hi].reshape(nb * tm, POOL_GROUP) + y * sc_ref[:, lo:hi])
    x1 = jnp.concatenate(mixed, axis=1)
    o_ref[...] = _swiglu_residual(x1, fg_ref, wg_ref, wu_ref, wd_ref).reshape(nb, tm, D_MODEL)


def _pool_ffn_kernel(xp_ref, xph_ref, xs_ref, hs_ref, *refs, prompt_tiles, tiles_per_seq):
    weights, (op_ref, hop_ref, os_ref, hos_ref) = refs[:7], refs[7:]
    i = pl.program_id(0)

    @pl.when(i < prompt_tiles)
    def _():
        j = i % tiles_per_seq
        hist = _rms(xph_ref[...], weights[0][...])
        hist = jnp.where(j == 0, jnp.zeros_like(hist), hist)
        _pool_ffn_tile(xp_ref, hist, j, weights, op_ref, hop_ref, full_hist=False)

    @pl.when(i == prompt_tiles)
    def _():
        _pool_ffn_tile(xs_ref, hs_ref[...], 0, weights, os_ref, hos_ref, full_hist=True)


def _pool_ffn_layer(xp, xs, hist_s, a_norm, w_pool, scale, ffn_gain, wg, wu, wd, layer, *, tm):
    b, s, d = xp.shape
    n, t, _ = xs.shape
    nt = s // tm
    steps = b * nt
    r = tm // HALO
    tile = lambda i: jnp.minimum(i, steps - 1)
    const2 = lambda *shape: pl.BlockSpec(shape, lambda i: (0,) * len(shape))
    return pl.pallas_call(
        functools.partial(_pool_ffn_kernel, prompt_tiles=steps, tiles_per_seq=nt),
        grid=(steps + 1,),
        in_specs=[
            pl.BlockSpec((1, tm, d), lambda i: (tile(i) // nt, tile(i) % nt, 0)),
            pl.BlockSpec((1, HALO, d), lambda i: (tile(i) // nt, jnp.maximum(tile(i) % nt * r - 1, 0), 0)),
            const2(n, t, d), const2(n, HALO, d),
            const2(1, d), const2(4, POOL_GROUP, POOL_GROUP), const2(1, d),
            *_ffn_weight_specs(layer),
        ],
        out_specs=[
            pl.BlockSpec((1, tm, d), lambda i: (tile(i) // nt, tile(i) % nt, 0)),
            pl.BlockSpec((1, HALO, d), lambda i: (tile(i) // nt, 0, 0)),
            const2(n, t, d), const2(n, HALO, d),
        ],
        out_shape=[jax.ShapeDtypeStruct((b, s, d), F32), jax.ShapeDtypeStruct((b, HALO, d), F32),
                   jax.ShapeDtypeStruct((n, t, d), F32), jax.ShapeDtypeStruct((n, HALO, d), F32)],
        compiler_params=_cparams(1),
        name="pool_ffn_layer",
    )(xp, xp, xs, hist_s, a_norm, w_pool, scale, ffn_gain[layer:layer + 1], wg, wu, wd)


def _ffn_kernel(xp_ref, xs_ref, g_ref, wg_ref, wu_ref, wd_ref, op_ref, os_ref, *, prompt_tiles):
    i = pl.program_id(0)

    @pl.when(i < prompt_tiles)
    def _():
        op_ref[...] = _swiglu_residual(xp_ref[...], g_ref, wg_ref, wu_ref, wd_ref)

    @pl.when(i == prompt_tiles)
    def _():
        os_ref[...] = _swiglu_residual(xs_ref[...], g_ref, wg_ref, wu_ref, wd_ref)


def _ffn(xp, xs, g, wg, wu, wd, layer, *, tm):
    (n_p, d), n_s = xp.shape, xs.shape[0]
    nt = n_p // tm
    p_tile = pl.BlockSpec((tm, d), lambda i: (jnp.minimum(i, nt - 1), 0))
    s_tile = pl.BlockSpec((n_s, d), lambda i: (0, 0))
    return pl.pallas_call(
        functools.partial(_ffn_kernel, prompt_tiles=nt),
        grid=(nt + 1,),
        in_specs=[p_tile, s_tile, *_ffn_weight_specs(layer)],
        out_specs=[p_tile, s_tile],
        out_shape=[jax.ShapeDtypeStruct((n_p, d), F32), jax.ShapeDtypeStruct((n_s, d), F32)],
        compiler_params=_cparams(1),
        name="ffn",
    )(xp, xs, g[layer:layer + 1], wg, wu, wd)


PROJ_COLS = N_BRANCH * ATTN_WIDTH


def _proj_dot(h, w_ref, c):
    return jnp.dot(h, w_ref[:, c * MXU_N:(c + 1) * MXU_N].astype(BF16), preferred_element_type=F32)


def _head_norm(y, c, hg_ref, m_ref):
    msq = jnp.dot((y * y).astype(BF16), m_ref[...], preferred_element_type=F32)
    return y * lax.rsqrt(msq + EPS) * hg_ref[:, c * MXU_N:(c + 1) * MXU_N]


def _rope_natural(y, cos, sa, sb):
    halves = []
    for p in range(MXU_N // LANES):
        z = y[:, p * LANES:(p + 1) * LANES]
        halves.append(z * cos + pltpu.roll(z, LANES - HEAD_DIM // 2, axis=1) * sa
                      + pltpu.roll(z, HEAD_DIM // 2, axis=1) * sb)
    return jnp.concatenate(halves, axis=1)


def _proj_sample_kernel(x_ref, g_ref, w_ref, hg_ref, cos_ref, sa_ref, sb_ref, m_ref, o_ref, *, n_rope):
    h = _rms(x_ref[...], g_ref[...]).astype(BF16)
    cos, sa, sb = cos_ref[...], sa_ref[...], sb_ref[...]
    for c in range(o_ref.shape[1] // MXU_N):
        y = _proj_dot(h, w_ref, c)
        if c * MXU_N < n_rope:
            y = _rope_natural(_head_norm(y, c, hg_ref, m_ref), cos, sa, sb)
        o_ref[:, c * MXU_N:(c + 1) * MXU_N] = y


def _proj_sample(x, g, w, w_index, hgain, tables, msum, *, n_rope):
    n, d = x.shape
    n_cols = w.shape[-1]
    full = lambda *shape: pl.BlockSpec(shape, lambda i: (0,) * len(shape))
    if w.ndim == 3:
        w_spec = pl.BlockSpec((None, d, n_cols), lambda i: (w_index, 0, 0))
    else:
        w_spec = full(d, n_cols)
    return pl.pallas_call(
        functools.partial(_proj_sample_kernel, n_rope=n_rope),
        grid=(1,),
        in_specs=[full(n, d), full(1, d), w_spec, full(1, n_rope),
                  full(n, LANES), full(n, LANES), full(n, LANES), full(MXU_N, MXU_N)],
        out_specs=full(n, n_cols),
        out_shape=jax.ShapeDtypeStruct((n, n_cols), F32),
        compiler_params=_cparams(1),
        name="proj_sample",
    )(x, g, w, hgain, *tables, msum)


def _proj_prompt_kernel(x_ref, g_ref, w_ref, hg_ref, cos_ref, sa_ref, sb_ref, m_ref, *refs,
                        tm, rope, tail_blocks, tail_start, tails_carried, zero_other_half):
    refs = refs[tails_carried:]
    outs, yscr = refs[:N_BRANCH], refs[-1]
    tails = refs[N_BRANCH:-1]
    j = pl.program_id(1)
    h = _rms(x_ref[...], g_ref[...]).astype(BF16)
    cos, sa, sb = cos_ref[...], sa_ref[...], sb_ref[...]
    per_branch = ATTN_WIDTH // MXU_N
    slabs = MXU_N // LANES
    n_chunks = PROJ_COLS // MXU_N
    y_next = _proj_dot(h, w_ref, 0)
    for c in range(n_chunks):
        g, cc = divmod(c, per_branch)
        y = y_next
        if c + 1 < n_chunks:
            y_next = _proj_dot(h, w_ref, c + 1)
        if rope:
            y = _rope_natural(_head_norm(y, c, hg_ref, m_ref), cos, sa, sb)
        dil = DILATIONS[g]
        if dil == 1:
            outs[g][0, 0, :, cc * MXU_N:(cc + 1) * MXU_N] = y.astype(BF16)
        if dil > 1 or tails:
            for p in range(slabs):
                yscr[c * slabs + p] = y[:, p * LANES:(p + 1) * LANES]
        if dil > 1:
            for r in range(dil):
                for p in range(slabs):
                    lo = cc * MXU_N + p * LANES
                    outs[g][0, r, :, lo:lo + LANES] = (
                        yscr[c * slabs + p, pl.ds(r, tm // dil, stride=dil), :].astype(BF16))
    for g in range(N_BRANCH if tails else 0):
        blk = tail_blocks[g]

        @pl.when(j >= tail_start[g])
        def _(g=g, blk=blk):
            for s in range(ATTN_WIDTH // LANES):
                slab = yscr[g * (ATTN_WIDTH // LANES) + s, tm - blk:, :]
                tails[g][0, 0, s * LANES:(s + 1) * LANES, :] = slab.T
            if zero_other_half:
                tails[g][0, 1] = jnp.zeros((ATTN_WIDTH, blk), F32)


def _proj_prompt(x, g, w, w_block, hgain, tables, msum, *, bsz, seq, tm, rope, tail_kv, tail_arrays):
    n, d = x.shape
    nt = seq // tm
    if w.ndim == 3:
        w_spec = pl.BlockSpec((None, d, PROJ_COLS), lambda b, j: (w_block, 0, 0), pipeline_mode=pl.Buffered(1))
    else:
        w_spec = pl.BlockSpec((d, PROJ_COLS), lambda b, j: (0, w_block), pipeline_mode=pl.Buffered(1))
    const = lambda *shape: pl.BlockSpec(shape, lambda b, j: (0,) * len(shape))
    tab = pl.BlockSpec((tm, LANES), lambda b, j: (j, 0))
    in_specs = [pl.BlockSpec((tm, d), lambda b, j: (b * nt + j, 0)), const(1, d), w_spec,
                const(1, PROJ_COLS), tab, tab, tab, const(MXU_N, MXU_N)]
    args = [x, g, w, hgain, *tables, msum]
    out_specs = [pl.BlockSpec((1, dil, tm // dil, ATTN_WIDTH), lambda b, j: (b, 0, j, 0)) for dil in DILATIONS]
    out_shape = [jax.ShapeDtypeStruct((bsz, dil, seq // dil, ATTN_WIDTH), BF16) for dil in DILATIONS]
    tail_blocks = tail_start = None
    aliases = {}
    if tail_kv is not None:
        create = tail_arrays is None
        assert create == (tail_kv == 0)
        keeps = [min(wd, seq) for wd in WINDOWS]
        tail_blocks = tuple(min(k, tm) for k in keeps)
        tail_start = tuple(nt - max(k // tm, 1) for k in keeps)
        for gi, (k, blk, j0) in enumerate(zip(keeps, tail_blocks, tail_start)):
            if create:
                spec = pl.BlockSpec((1, 2, ATTN_WIDTH, blk), lambda b, j, j0=j0: (b, 0, 0, jnp.maximum(j - j0, 0)))
            else:
                spec = pl.BlockSpec((1, 1, ATTN_WIDTH, blk),
                                    lambda b, j, j0=j0: (b, tail_kv, 0, jnp.maximum(j - j0, 0)))
                aliases[len(args)] = N_BRANCH + gi
                args.append(tail_arrays[gi])
                in_specs.append(pl.BlockSpec(memory_space=pl.ANY))
            out_specs.append(spec)
            out_shape.append(jax.ShapeDtypeStruct((bsz, 2, ATTN_WIDTH, k), F32))
    kern = functools.partial(_proj_prompt_kernel, tm=tm, rope=rope, tail_blocks=tail_blocks, tail_start=tail_start,
                             tails_carried=len(aliases), zero_other_half=tail_kv == 0)
    return pl.pallas_call(
        kern,
        grid=(bsz, nt),
        in_specs=in_specs,
        out_specs=out_specs,
        out_shape=out_shape,
        scratch_shapes=[pltpu.VMEM((PROJ_COLS // LANES, tm, LANES), F32)],
        input_output_aliases=aliases,
        compiler_params=_cparams(2),
        name="proj_prompt",
    )(*args)


ATTN_Q_BLOCKS = 8
LN2 = float(np.log(2.0))
LOG2E = float(1.0 / np.log(2.0))


def _lse_lane(head):
    return head if head % 2 == 0 else HEAD_DIM + head


def _attn_kernel(q_ref, kp_ref, kc_ref, vp_ref, vc_ref, bias_ref, o_ref, lse_ref):
    j = pl.program_id(1)
    lane = lax.broadcasted_iota(jnp.int32, (BLK, LANES), 1)
    first = lane < HEAD_DIM
    nt = (((1,), (1,)), ((), ()))
    bias_first = bias_ref[jnp.minimum(j, 1)]

    def keys_values(s, b, p, ref_prev, ref_cur):
        sl = slice(p * LANES, (p + 1) * LANES)
        prev = ref_prev[s, :, sl] if b == 0 else ref_cur[s, (b - 1) * BLK:b * BLK, sl]
        return jnp.concatenate([prev, ref_cur[s, b * BLK:(b + 1) * BLK, sl]], axis=0)

    def scores(s, b, p):
        q2 = q_ref[s, b * BLK:(b + 1) * BLK, p * LANES:(p + 1) * LANES]
        zero = jnp.zeros_like(q2)
        qq = jnp.concatenate([jnp.where(first, q2, zero), jnp.where(first, zero, q2)], axis=0)
        kk = keys_values(s, b, p, kp_ref, kc_ref)
        return lax.dot_general(qq, kk, nt, preferred_element_type=F32) + (bias_first if b == 0 else bias_ref[1])

    n_seq, n_blk = q_ref.shape[0], q_ref.shape[1] // BLK
    pairs = [(s, b, p) for s in range(n_seq) for b in range(n_blk) for p in range(N_HEADS // 2)]
    s_next = scores(*pairs[0])
    lse_all = None
    for i, (sq, b, p) in enumerate(pairs):
        s = s_next
        if i + 1 < len(pairs):
            s_next = scores(*pairs[i + 1])
        m = jnp.max(s, axis=-1, keepdims=True)
        e = jnp.exp2(s - m).astype(BF16)
        vv = keys_values(sq, b, p, vp_ref, vc_ref)
        v1 = jnp.concatenate([vv, jnp.ones_like(vv)], axis=1)
        pv = jnp.dot(e, v1, preferred_element_type=F32)
        rows, sl = slice(b * BLK, (b + 1) * BLK), slice(p * LANES, (p + 1) * LANES)
        num = jnp.where(first, pv[:BLK, :LANES], pv[BLK:, :LANES])
        den = jnp.where(first, pv[:BLK, LANES:], pv[BLK:, LANES:])
        o_ref[sq, rows, sl] = (num / den).astype(o_ref.dtype)
        lse_pair = jnp.where(first, m[:BLK], m[BLK:]) * LN2 + jnp.log(den)
        here = (lane == _lse_lane(2 * p)) | (lane == _lse_lane(2 * p + 1))
        lse_all = jnp.where(here, lse_pair, jnp.zeros_like(lse_pair) if p == 0 else lse_all)
        if p == N_HEADS // 2 - 1:
            lse_ref[sq, rows, :] = lse_all


def _attn(q, k, v, bias):
    ns, l, w = q.shape
    nq = min(ATTN_Q_BLOCKS, l // BLK)
    nsq = ATTN_Q_BLOCKS // nq
    cur = lambda s, j: (s, j, 0)
    prev = lambda s, j: (s, jnp.maximum(j * nq - 1, 0), 0)
    big, small = (nsq, nq * BLK, w), (nsq, BLK, w)
    return pl.pallas_call(
        _attn_kernel,
        grid=(ns // nsq, l // (nq * BLK)),
        in_specs=[
            pl.BlockSpec(big, cur),
            pl.BlockSpec(small, prev), pl.BlockSpec(big, cur),
            pl.BlockSpec(small, prev), pl.BlockSpec(big, cur),
            pl.BlockSpec((2, 2 * BLK, 2 * BLK), lambda s, j: (0, 0, 0)),
        ],
        out_specs=[pl.BlockSpec(big, cur), pl.BlockSpec((nsq, nq * BLK, LANES), cur)],
        out_shape=[jax.ShapeDtypeStruct((ns, l, w), BF16), jax.ShapeDtypeStruct((ns, l, LANES), F32)],
        compiler_params=_cparams(2),
        name="attn_prompt",
    )(q, k, k, v, v, bias)


SAMPLE_CACHE_BLOCK_POSITIONS = 16384


REQUESTS_PER_NEW_BLOCK = LANES // 8


def _attn_sample_kernel(q_ref, cache_ref, new_ref, bias_c_ref, o_ref, lse_ref, *cache_out, hist, dil):
    nr, _, hb = cache_ref.shape[:3]
    lane = lax.broadcasted_iota(jnp.int32, (8, LANES), 1)
    t_q = lax.broadcasted_iota(jnp.int32, (8, LANES), 0)
    bias_c = bias_c_ref[...]
    lane_c = lax.broadcasted_iota(jnp.int32, (HEAD_DIM, LANES), 1)
    nt = (((1,), (1,)), ((), ()))
    units = [(r, h) for r in range(nr) for h in range(hb)]
    slots, bias_ns = [], []
    for r in range(nr):
        n = pl.program_id(0) * nr + r
        slot = (n % REQUESTS_PER_NEW_BLOCK) * 8
        diff = t_q - (lane - slot)
        ok = (diff >= 0) & (diff <= t_q) & ((diff & (dil - 1)) == 0)
        slots.append(slot)
        bias_ns.append(jnp.where(ok, 0.0, NEG))
    s_cs = [jnp.dot(q_ref[r, h], cache_ref[r, 0, h], preferred_element_type=F32) + bias_c for r, h in units]
    s_ns = [jnp.dot(q_ref[r, h], new_ref[0, h], preferred_element_type=F32) + bias_ns[r] for r, h in units]
    ms = [jnp.maximum(jnp.max(s_c, axis=-1, keepdims=True), jnp.max(s_n, axis=-1, keepdims=True))
          for s_c, s_n in zip(s_cs, s_ns)]
    e_cs = [jnp.exp(s_c - m) for s_c, m in zip(s_cs, ms)]
    e_ns = [jnp.exp(s_n - m) for s_n, m in zip(s_ns, ms)]
    ls = [jnp.sum(e_c, axis=-1, keepdims=True) + jnp.sum(e_n, axis=-1, keepdims=True)
          for e_c, e_n in zip(e_cs, e_ns)]
    for i, (r, h) in enumerate(units):
        o = (lax.dot_general(e_cs[i], cache_ref[r, 1, h], nt, preferred_element_type=F32)
             + lax.dot_general(e_ns[i], new_ref[1, h], nt, preferred_element_type=F32))
        o_ref[r, h] = o / ls[i]
        lse_ref[r, h] = jnp.broadcast_to(ms[i] + jnp.log(ls[i]), (8, HEAD_DIM))
    for out_ref in cache_out:
        for r, h in units:
            for kv in range(2):
                shifted = pltpu.roll(cache_ref[r, kv, h], hist - 8, axis=1)
                if hist > LANES:
                    out_ref[r, kv, h, :, :hist - LANES] = shifted[:, :hist - LANES]
                moved = pltpu.roll(new_ref[kv, h], (LANES - 8) - slots[r], axis=1)
                out_ref[r, kv, h, :, hist - LANES:] = jnp.where(
                    lane_c >= LANES - 8, moved, shifted[:, hist - LANES:])


def _attn_sample(q, cache, new_t, bias_c, branch, *, dil, write_cache):
    nreq = q.shape[1]
    hist = cache.shape[-1]
    hb = min(N_HEADS, SAMPLE_CACHE_BLOCK_POSITIONS // hist)
    nr = min(REQUESTS_PER_NEW_BLOCK, max(1, SAMPLE_CACHE_BLOCK_POSITIONS // (hb * hist)))
    o_spec = pl.BlockSpec((nr, hb, 8, HEAD_DIM), lambda n, h: (n, h, 0, 0))
    out_specs = [o_spec, o_spec]
    out_shape = [jax.ShapeDtypeStruct((nreq, N_HEADS, 8, HEAD_DIM), F32)] * 2
    if write_cache:
        out_specs.append(pl.BlockSpec((nr, 2, hb, HEAD_DIM, hist), lambda n, h: (n, 0, h, 0, 0)))
        out_shape.append(jax.ShapeDtypeStruct(cache.shape, F32))
    return pl.pallas_call(
        functools.partial(_attn_sample_kernel, hist=hist, dil=dil),
        grid=(nreq // nr, N_HEADS // hb),
        in_specs=[
            pl.BlockSpec((None, nr, hb, 8, HEAD_DIM), lambda n, h: (branch, n, h, 0, 0)),
            pl.BlockSpec((nr, 2, hb, HEAD_DIM, hist), lambda n, h: (n, 0, h, 0, 0)),
            pl.BlockSpec((2, None, hb, HEAD_DIM, LANES),
                         lambda n, h: (0, branch, h, 0, n * nr // REQUESTS_PER_NEW_BLOCK)),
            pl.BlockSpec((8, hist), lambda n, h: (0, 0)),
        ],
        out_specs=out_specs,
        out_shape=out_shape,
        compiler_params=_cparams(2),
        name="attn_sample",
    )(q, cache, new_t, bias_c)


def _merge_sample_kernel(x_ref, o0, o1, o2, l0, l1, l2, w_ref, out_ref):
    ls = [l0[...], l1[...], l2[...]]
    m = jnp.maximum(jnp.maximum(ls[0], ls[1]), ls[2])
    es = [jnp.exp(l - m) for l in ls]
    den = es[0] + es[1] + es[2]
    num = es[0] * o0[...] + es[1] * o1[...] + es[2] * o2[...]
    merged = (num / den).astype(BF16)
    out_ref[...] = x_ref[...] + jnp.dot(merged, w_ref[...].astype(BF16), preferred_element_type=F32)


def _merge_sample(x, outs, lses, w_o, layer):
    n, d = x.shape
    tile = pl.BlockSpec((n, d), lambda i: (0, 0))
    return pl.pallas_call(
        _merge_sample_kernel,
        grid=(1,),
        in_specs=[tile] * 7 + [pl.BlockSpec((None, d, d), lambda i: (layer, 0, 0))],
        out_specs=tile,
        out_shape=jax.ShapeDtypeStruct((n, d), F32),
        compiler_params=_cparams(1),
        name="merge_sample",
    )(x, *outs, *lses, w_o)


WEIGHT_PART_LANES = N_HEADS


def _merge_prompt_kernel(x_ref, o0_ref, o1_ref, o2_ref, l0_ref, l1_ref, l2_ref, e_ref, w_ref, out_ref,
                         oscr1, oscr2, lscr1, lscr2, *, tm):
    o_refs, l_refs = (o0_ref, o1_ref, o2_ref), (l0_ref, l1_ref, l2_ref)
    o_scr, l_scr = (None, oscr1, oscr2), (None, lscr1, lscr2)
    os_, ls_ = [], []
    for g, dil in enumerate(DILATIONS):
        if dil == 1:
            os_.append(o_refs[g][0, 0].astype(F32))
            ls_.append(l_refs[g][0, 0])
        else:
            for r in range(dil):
                rows = pl.ds(r, tm // dil, stride=dil)
                o_r = o_refs[g][0, r].astype(F32)
                for c in range(ATTN_WIDTH // LANES):
                    o_scr[g][c, rows, :] = o_r[:, c * LANES:(c + 1) * LANES]
                l_scr[g][rows, :] = l_refs[g][0, r]
            os_.append(jnp.concatenate([o_scr[g][c] for c in range(ATTN_WIDTH // LANES)], axis=1))
            ls_.append(l_scr[g][...])
    m = jnp.maximum(jnp.maximum(ls_[0], ls_[1]), ls_[2])
    es = [jnp.exp(l - m) for l in ls_]
    den = es[0] + es[1] + es[2]
    lane = lax.broadcasted_iota(jnp.int32, (1, LANES), 1)
    head_lane = functools.reduce(jnp.logical_or, [lane == _lse_lane(h) for h in range(N_HEADS)])
    packed = None
    for g in range(N_BRANCH):
        w = jnp.where(head_lane, es[g] / den, 0.0)
        hi = w.astype(BF16).astype(F32)
        lo = (w - hi).astype(BF16).astype(F32)
        for part, val in ((2 * g, hi), (2 * g + 1, lo)):
            if part:
                val = pltpu.roll(val, part * WEIGHT_PART_LANES, axis=1)
            packed = val if packed is None else packed + val
    spread = jnp.dot(packed.astype(BF16), e_ref[...], preferred_element_type=F32)
    merged = sum(spread[:, g * ATTN_WIDTH:(g + 1) * ATTN_WIDTH] * os_[g] for g in range(N_BRANCH))
    out_ref[...] = x_ref[...] + jnp.dot(merged.astype(BF16), w_ref[...].astype(BF16), preferred_element_type=F32)


def _merge_prompt(x, outs, lses, spread_mat, w_o, layer, *, bsz, seq, tm):
    n, d = x.shape
    nt = seq // tm
    tile = pl.BlockSpec((tm, d), lambda b, j: (b * nt + j, 0))
    cls = lambda width: [pl.BlockSpec((1, dil, tm // dil, width), lambda b, j: (b, 0, j, 0))
                         for dil in DILATIONS]
    return pl.pallas_call(
        functools.partial(_merge_prompt_kernel, tm=tm),
        grid=(bsz, nt),
        in_specs=[tile, *cls(ATTN_WIDTH), *cls(LANES),
                  pl.BlockSpec((LANES, N_BRANCH * ATTN_WIDTH), lambda b, j: (0, 0)),
                  pl.BlockSpec((None, d, d), lambda b, j: (layer, 0, 0))],
        out_specs=tile,
        out_shape=jax.ShapeDtypeStruct((n, d), F32),
        scratch_shapes=[pltpu.VMEM((ATTN_WIDTH // LANES, tm, LANES), F32),
                        pltpu.VMEM((ATTN_WIDTH // LANES, tm, LANES), F32),
                        pltpu.VMEM((tm, LANES), F32), pltpu.VMEM((tm, LANES), F32)],
        compiler_params=_cparams(2),
        name="merge_prompt",
    )(x, *outs, *lses, spread_mat, w_o)


def _rope_tables(pos):
    inv = jnp.power(ROPE_THETA, -jnp.arange(0, HEAD_DIM, 2, dtype=F32) / HEAD_DIM)
    ang = pos.astype(F32)[:, None] * inv[None, :]
    ang = jnp.concatenate([ang, ang], axis=-1)
    cos, sin = jnp.cos(ang), jnp.sin(ang)
    first = jnp.arange(HEAD_DIM) < HEAD_DIM // 2
    sa = jnp.where(first, -sin, 0.0)
    sb = jnp.where(first, 0.0, sin)
    tile2 = lambda a: jnp.concatenate([a, a], axis=-1)
    return tile2(cos), tile2(sa), tile2(sb)


def _head_mean_matrix():
    head = np.arange(MXU_N) // HEAD_DIM
    return jnp.asarray((head[:, None] == head[None, :]) / HEAD_DIM, dtype=BF16)


def _prompt_bias():
    qi = np.arange(BLK)[:, None]
    ki = np.arange(2 * BLK)[None, :]
    rel = qi + BLK - ki
    band = (rel >= 0) & (rel <= BLK)
    first = band & (ki >= BLK)
    both = np.stack([np.tile(first, (2, 1)), np.tile(band, (2, 1))])
    return jnp.asarray(np.where(both, 0.0, NEG), dtype=F32)


def _sample_bias(hist, dil):
    t = np.arange(8)[:, None]
    l = np.arange(hist)[None, :]
    ok_c = (l >= t) & ((l - t) % dil == 0)
    return jnp.asarray(np.where(ok_c, 0.0, NEG), dtype=F32)


def _weight_spread_matrix():
    hit = np.zeros((LANES, N_BRANCH * ATTN_WIDTH), np.float32)
    used = set()
    for part in range(2 * N_BRANCH):
        for h in range(N_HEADS):
            lane = (_lse_lane(h) + part * WEIGHT_PART_LANES) % LANES
            assert lane not in used
            used.add(lane)
            col = (part // 2) * ATTN_WIDTH + h * HEAD_DIM
            hit[lane, col:col + HEAD_DIM] = 1.0
    return jnp.asarray(hit, dtype=BF16)


def kernel(x_prompt, x_sample, state_pool, cache_kv_w128, cache_kv_w512, cache_kv_w2048,
           a_norm, pool_w, pool_scale, kv_norm, w_kv, k_norm, b_norm, w_q, q_norm, w_o,
           ffn_norm, w_gate, w_up, w_down):
    bsz, seq, d = x_prompt.shape
    nreq, t_new, _ = x_sample.shape
    caches = (cache_kv_w128, cache_kv_w512, cache_kv_w2048)
    n_p, n_s = bsz * seq, nreq * t_new

    pool_w_b = pool_w.astype(BF16)
    w_kv_b, w_q_b, w_o_b = w_kv, w_q, w_o
    wg_b, wu_b, wd_b = w_gate, w_up, w_down

    tab_p = _rope_tables(jnp.arange(seq))
    tab_s = tuple(jnp.tile(a, (nreq, 1)) for a in _rope_tables(PAST_LEN + jnp.arange(t_new)))
    msum = _head_mean_matrix()
    k_gain = jnp.tile(k_norm[:, None, :], (1, N_HEADS, 1)).reshape(1, PROJ_COLS)
    bias_p = _prompt_bias()

    xp, xs = x_prompt, x_sample
    pool_p, pool_s = [], []
    for i in range(N_A_LAYERS):
        hist_s = jnp.pad(state_pool[:, i], ((0, 0), (1, 0), (0, 0)))
        xp, hp, xs, hs = _pool_ffn_layer(xp, xs, hist_s, a_norm[i:i + 1], pool_w_b[i], pool_scale[i:i + 1],
                                         ffn_norm, wg_b, wu_b, wd_b, i, tm=TM_PROMPT)
        pool_p.append(hp[:, 1:])
        pool_s.append(hs[:, 1:])

    xp, xs = xp.reshape(n_p, d), xs.reshape(n_s, d)
    kv_norm2 = kv_norm.reshape(1, d)
    prompt = dict(bsz=bsz, seq=seq, tm=TM_PROMPT)
    k_out = _proj_prompt(xp, kv_norm2, w_kv_b, 0, k_gain, tab_p, msum, rope=True,
                         tail_kv=0, tail_arrays=None, **prompt)
    v_out = _proj_prompt(xp, kv_norm2, w_kv_b, 1, k_gain, tab_p, msum, rope=False,
                         tail_kv=1, tail_arrays=k_out[N_BRANCH:], **prompt)
    k_cls, v_cls = k_out[:N_BRANCH], v_out[:N_BRANCH]
    kv_new_p = [t.reshape(bsz, 2, N_HEADS, HEAD_DIM, t.shape[-1]).transpose(0, 4, 1, 2, 3)
                for t in v_out[N_BRANCH:]]
    k_cls = [a.reshape(bsz * dil, seq // dil, ATTN_WIDTH) for a, dil in zip(k_cls, DILATIONS)]
    v_cls = [a.reshape(bsz * dil, seq // dil, ATTN_WIDTH) for a, dil in zip(v_cls, DILATIONS)]

    kv_s = _proj_sample(xs, kv_norm2, w_kv_b, 0, k_gain, tab_s, msum, n_rope=PROJ_COLS)
    caches_t = [c.transpose(0, 2, 3, 4, 1) for c in caches]
    new_t = kv_s.T.reshape(2, N_BRANCH, N_HEADS, HEAD_DIM, n_s)
    sample_bias = [_sample_bias(c.shape[1], dil) for c, dil in zip(caches, DILATIONS)]
    spread_mat = _weight_spread_matrix()

    kv_sample_out = [None] * N_BRANCH
    for j in range(N_B_LAYERS):
        layer = N_A_LAYERS + j
        q_gain = jnp.tile(q_norm[j][:, None, :], (1, N_HEADS, 1)).reshape(1, PROJ_COLS)
        q_gain = q_gain * (HEAD_DIM ** -0.5)
        bn = b_norm[j:j + 1]
        q_cls = _proj_prompt(xp, bn, w_q_b, j, q_gain * LOG2E, tab_p, msum, rope=True,
                             tail_kv=None, tail_arrays=None, **prompt)
        q_s = _proj_sample(xs, bn, w_q_b, j, q_gain, tab_s, msum, n_rope=PROJ_COLS)
        q_s5 = q_s.reshape(nreq, t_new, N_BRANCH, N_HEADS, HEAD_DIM).transpose(2, 0, 3, 1, 4)

        outs_p, lses_p, outs_s, lses_s = [], [], [], []
        for g, dil in enumerate(DILATIONS):
            q_c = q_cls[g].reshape(bsz * dil, seq // dil, ATTN_WIDTH)
            o_c, lse_c = _attn(q_c, k_cls[g], v_cls[g], bias_p)
            outs_p.append(o_c.reshape(bsz, dil, seq // dil, ATTN_WIDTH))
            lses_p.append(lse_c.reshape(bsz, dil, seq // dil, LANES))

            res = _attn_sample(q_s5, caches_t[g], new_t, sample_bias[g], g, dil=dil, write_cache=(j == 0))
            outs_s.append(res[0].transpose(0, 2, 1, 3).reshape(n_s, ATTN_WIDTH))
            lses_s.append(res[1].transpose(0, 2, 1, 3).reshape(n_s, ATTN_WIDTH))
            if j == 0:
                kv_sample_out[g] = res[2].transpose(0, 4, 1, 2, 3)

        xp = _merge_prompt(xp, outs_p, lses_p, spread_mat, w_o_b, j, **prompt)
        xs = _merge_sample(xs, outs_s, lses_s, w_o_b, j)
        xp, xs = _ffn(xp, xs, ffn_norm, wg_b, wu_b, wd_b, layer, tm=TM_PROMPT)

    return (xp.reshape(bsz, seq, d), xs.reshape(nreq, t_new, d),
            jnp.stack(pool_p, axis=1), jnp.stack(pool_s, axis=1),
            kv_new_p[0], kv_sample_out[0], kv_new_p[1], kv_sample_out[1],
            kv_new_p[2], kv_sample_out[2])
```
